```python
import math
import jax
import jax.numpy as jnp
from jax import lax
import numpy as np

D_MODEL = 1024
BATCH = 4
SEQ = 4096
DEPTH = 2
DEC_BATCH = 32
DEC_SEQ = 1
PAST_LEN = 16384
PAGE_SIZE = 128

DN_H = 4
DN_DK = 128
DN_DV = 128
DN_QK = DN_H * DN_DK
DN_W = DN_H * DN_DV
DN_CONV_C = 2 * DN_QK + DN_W
DN_CONV = 4
DN_CHUNK = 64
CF_C = 512
CF_K = 31
NSA_H = 8
NSA_G = 2
NSA_HPG = NSA_H // NSA_G
NSA_DH = 64
NSA_W = NSA_H * NSA_DH
NSA_KV = 3 * 2 * NSA_G * NSA_DH
L_CMP = 32
L_SEL = 64
N_TOP = 16
WINDOW = 512
Q_BLOCK = 128
FORCE_SCORE = 1.0e4
D_FF = int(math.ceil(8 * D_MODEL / 3 / 256)) * 256
DEEPNORM_ALPHA = (2 * DEPTH) ** 0.25
DEEPNORM_BETA = (8 * DEPTH) ** -0.25
OFF_DN_QKV = 0
OFF_DN_Z = OFF_DN_QKV + DN_CONV_C
OFF_DN_B = OFF_DN_Z + DN_W
OFF_DN_A = OFF_DN_B + DN_H
OFF_CF = OFF_DN_A + DN_H
OFF_NSA_Q = OFF_CF + 2 * CF_C
OFF_NSA_KV = OFF_NSA_Q + NSA_W
OFF_NSA_G = OFF_NSA_KV + NSA_KV
OFF_MERGE = OFF_NSA_G + 3 * NSA_H
N_IN = OFF_MERGE + 3 * D_MODEL

F32 = jnp.float32

kernel_name = 'hybrid_deltanet_conformer_nsa_step'


def layer_norm(x, g, b, eps=1e-5):
    xf = x.astype(F32)
    xc = xf - jnp.mean(xf, -1, keepdims=True)
    var = jnp.mean(xc * xc, -1, keepdims=True)
    return (xc * lax.rsqrt(var + eps) * g + b).astype(x.dtype)


def l2norm(x, eps=1e-6):
    return x * lax.rsqrt(jnp.sum(x * x, -1, keepdims=True) + eps)


def masked_softmax(s, mask):
    s = jnp.where(mask, s.astype(F32), -jnp.inf)
    m = jnp.max(s, axis=-1, keepdims=True)
    m = jnp.where(jnp.isfinite(m), m, 0.0)
    e = jnp.exp(s - m)
    den = jnp.sum(e, -1, keepdims=True)
    return e / jnp.where(den > 0, den, 1.0)


def causal_dwconv(xp, w):
    return lax.conv_general_dilated(xp, w.astype(xp.dtype)[:, None, :], window_strides=(1,), padding='VALID',
                                    dimension_numbers=('NWC', 'WIO', 'NWC'), feature_group_count=xp.shape[-1])


def dn_features(conv_out, b_raw, a_raw, a_log, dt_bias):
    h = jax.nn.silu(conv_out.astype(F32))
    B, T, _ = h.shape
    q = l2norm(h[..., :DN_QK].reshape(B, T, DN_H, DN_DK)) * (DN_DK ** -0.5)
    k = l2norm(h[..., DN_QK:2 * DN_QK].reshape(B, T, DN_H, DN_DK))
    v = h[..., 2 * DN_QK:].reshape(B, T, DN_H, DN_DV)
    beta = jax.nn.sigmoid(b_raw.astype(F32))
    g = -jnp.exp(a_log.astype(F32)) * jax.nn.softplus(a_raw.astype(F32) + dt_bias.astype(F32))
    return q, k, v, beta, g


def gated_delta_chunked(q, k, v, beta, g):
    B, T, H, DK = q.shape
    DV = v.shape[-1]
    C = DN_CHUNK
    N = T // C

    def chunks(a):
        return jnp.moveaxis(a.reshape((B, N, C, H) + a.shape[3:]), 2, 3)

    qc, kc, vc, bc, gc = chunks(q), chunks(k), chunks(v), chunks(beta), chunks(g)
    G = jnp.cumsum(gc, axis=-1)
    i = jnp.arange(C)
    strict = i[:, None] > i[None, :]
    incl = i[:, None] >= i[None, :]
    diff = G[..., :, None] - G[..., None, :]
    A = bc[..., :, None] * jnp.einsum('bnhid,bnhjd->bnhij', kc, kc) * jnp.exp(jnp.where(strict, diff, -jnp.inf))
    rhs = jnp.concatenate([bc[..., None] * vc, (bc * jnp.exp(G))[..., None] * kc], axis=-1)
    X = lax.linalg.triangular_solve(A + jnp.eye(C, dtype=A.dtype), rhs, left_side=True, lower=True, unit_diagonal=True)
    w_val, k_cum = X[..., :DV], X[..., DV:]
    qk = jnp.einsum('bnhid,bnhjd->bnhij', qc, kc) * jnp.exp(jnp.where(incl, diff, -jnp.inf))
    q_dec = qc * jnp.exp(G)[..., None]
    k_dec = kc * jnp.exp(G[..., -1:] - G)[..., None]
    g_last = jnp.exp(G[..., -1])

    def step(S, xs):
        w_i, kcum_i, qk_i, qd_i, kd_i, gl_i = xs
        u = w_i - jnp.einsum('bhik,bhkv->bhiv', kcum_i, S)
        o = jnp.einsum('bhik,bhkv->bhiv', qd_i, S) + jnp.einsum('bhij,bhjv->bhiv', qk_i, u)
        S = gl_i[..., None, None] * S + jnp.einsum('bhik,bhiv->bhkv', kd_i, u)
        return S, o

    xs = tuple(jnp.moveaxis(a, 1, 0) for a in (w_val, k_cum, qk, q_dec, k_dec, g_last))
    S, o = lax.scan(step, jnp.zeros((B, H, DK, DV), F32), xs)
    return jnp.transpose(o, (1, 0, 3, 2, 4)).reshape(B, T, H, DV), S


def gated_delta_recurrent(S0, q, k, v, beta, g):
    def step(S, xs):
        q_t, k_t, v_t, b_t, g_t = xs
        S = jnp.exp(g_t)[..., None, None] * S
        u = b_t[..., None] * (v_t - jnp.einsum('bhkv,bhk->bhv', S, k_t))
        S = S + jnp.einsum('bhk,bhv->bhkv', k_t, u)
        return S, jnp.einsum('bhkv,bhk->bhv', S, q_t)

    S, o = lax.scan(step, S0, tuple(jnp.moveaxis(a, 1, 0) for a in (q, k, v, beta, g)))
    return jnp.moveaxis(o, 0, 1), S


def dn_output(o, z, norm_g):
    B, T = o.shape[:2]
    o = o * lax.rsqrt(jnp.mean(o * o, -1, keepdims=True) + 1e-6) * norm_g.astype(F32)
    o = o * jax.nn.silu(z.astype(F32).reshape(B, T, DN_H, DN_DV))
    return o.reshape(B, T, DN_W)


def cf_glu(u_cf):
    a, gt = jnp.split(u_cf, 2, axis=-1)
    return a * jax.nn.sigmoid(gt)


def cf_tail(glu_padded, lp):
    h = causal_dwconv(glu_padded, lp['cf_dw_w']) + lp['cf_dw_b']
    return jax.nn.silu(layer_norm(h, lp['cf_ln_g'], lp['cf_ln_b']))


def nsa_split(u):
    B, T, _ = u.shape
    q = u[..., OFF_NSA_Q:OFF_NSA_Q + NSA_W].reshape(B, T, NSA_G, NSA_HPG, NSA_DH)
    kv = u[..., OFF_NSA_KV:OFF_NSA_KV + NSA_KV].reshape(B, T, 3, 2, NSA_G, NSA_DH)
    gates = jax.nn.sigmoid(u[..., OFF_NSA_G:OFF_NSA_G + 3 * NSA_H].astype(F32)).reshape(B, T, NSA_G, NSA_HPG, 3)
    return q, kv[:, :, 0], kv[:, :, 1], kv[:, :, 2], gates


def nsa_compress(rows, pe, w1, w2):
    B, T = rows.shape[:2]
    nc = T // L_CMP
    blk = rows[:, :nc * L_CMP].reshape(B, nc, L_CMP, 2, NSA_G, NSA_DH).astype(F32)
    blk = blk + pe.astype(F32).transpose(1, 0, 2)[:, :, None, :]
    flat = blk.transpose(0, 1, 3, 4, 2, 5).reshape(B, nc, 2, NSA_G, L_CMP * NSA_DH)
    h = jax.nn.silu(jnp.einsum('bnsgf,sfd->bnsgd', flat, w1.astype(F32)))
    return jnp.einsum('bnsgd,sde->bnsge', h, w2.astype(F32))


def make_rows_gather(rows):
    B = rows.shape[0]
    blocks = rows.reshape(B, -1, L_SEL, 2, NSA_G, NSA_DH)
    b_idx = jnp.arange(B)[:, None, None, None]
    g_idx = jnp.arange(NSA_G)[None, None, :, None]

    def gather(idx):
        return blocks[b_idx, idx, :, :, g_idx, :]
    return gather


def make_paged_gather(pool, page_table, new_rows):
    n_pool = pool.shape[0]
    sub = PAGE_SIZE // L_SEL
    pool_b = pool.reshape(n_pool, sub, L_SEL, 2, NSA_G, NSA_DH)
    DB, n_pages = page_table.shape
    n_past_blk = n_pages * sub
    Tn = new_rows.shape[1]
    n_new_blk = -(-Tn // L_SEL)
    new_b = jnp.pad(new_rows.astype(pool.dtype), ((0, 0), (0, n_new_blk * L_SEL - Tn), (0, 0), (0, 0), (0, 0)))
    new_b = new_b.reshape(DB, n_new_blk, L_SEL, 2, NSA_G, NSA_DH)
    b_idx = jnp.arange(DB)[:, None, None, None]
    g_idx = jnp.arange(NSA_G)[None, None, :, None]

    def gather(idx):
        pi = jnp.clip(idx, 0, n_past_blk - 1)
        phys = page_table[b_idx, pi // sub]
        from_pages = pool_b[phys, pi % sub, :, :, g_idx, :]
        from_new = new_b[b_idx, jnp.clip(idx - n_past_blk, 0, n_new_blk - 1), :, :, g_idx, :]
        return jnp.where((idx < n_past_blk)[..., None, None, None], from_pages, from_new)
    return gather, n_past_blk + n_new_blk


def nsa_core(q, t, gates, kvc, gather, n_sel, kv_win, s_win):
    B, Tq = q.shape[:2]
    qf = q.astype(F32) * (NSA_DH ** -0.5)
    nc = kvc.shape[1]
    s_c = jnp.einsum('bqghd,bcgd->bqghc', qf, kvc[:, :, 0])
    c_end = (jnp.arange(nc) + 1) * L_CMP - 1
    p_c = masked_softmax(s_c, (c_end[None, :] <= t[:, None])[None, :, None, None, :])
    o_cmp = jnp.einsum('bqghc,bcgd->bqghd', p_c, kvc[:, :, 1])
    ratio = L_SEL // L_CMP
    imp = jnp.sum(p_c, axis=3)
    imp = jnp.pad(imp, ((0, 0), (0, 0), (0, 0), (0, n_sel * ratio - nc))).reshape(B, Tq, NSA_G, n_sel, ratio).sum(-1)
    blk = jnp.arange(n_sel)[None, :]
    cur = (t // L_SEL)[:, None]
    valid = blk * L_SEL <= t[:, None]
    forced = (blk == 0) | (blk == cur) | (blk == cur - 1)
    score = jnp.where(valid[None, :, None, :], jnp.where(forced[None, :, None, :], FORCE_SCORE, imp), -jnp.inf)
    top_val, idx = lax.top_k(score, min(N_TOP, n_sel))
    kv_sel = gather(idx)
    kk = idx.shape[-1]
    s_s = jnp.einsum('bqghd,bqgkld->bqghkl', qf, kv_sel[..., 0, :])
    pos = idx[..., None] * L_SEL + jnp.arange(L_SEL)
    m_s = jnp.isfinite(top_val)[..., None] & (pos <= t[None, :, None, None, None])
    p_s = masked_softmax(s_s.reshape(B, Tq, NSA_G, NSA_HPG, kk * L_SEL),
                         m_s.reshape(B, Tq, NSA_G, 1, kk * L_SEL)).reshape(s_s.shape)
    o_slc = jnp.einsum('bqghkl,bqgkld->bqghd', p_s, kv_sel[..., 1, :])
    s_w = jnp.einsum('bqghd,bwgd->bqghw', qf, kv_win[:, :, 0])
    dist = t[:, None] - s_win[None, :]
    m_w = (dist >= 0) & (dist <= WINDOW) & (s_win[None, :] >= 0)
    p_w = masked_softmax(s_w, m_w[None, :, None, None, :])
    o_win = jnp.einsum('bqghw,bwgd->bqghd', p_w, kv_win[:, :, 1])
    return gates[..., 0:1] * o_cmp + gates[..., 1:2] * o_slc + gates[..., 2:3] * o_win


def nsa_prompt(q, kv_cmp, kv_slc, kv_win, gates, lp):
    B, T = q.shape[:2]
    kvc = nsa_compress(kv_cmp, lp['nsa_cmp_pe'], lp['nsa_cmp_w1'], lp['nsa_cmp_w2'])
    gather = make_rows_gather(kv_slc)
    n_sel = T // L_SEL
    kvw_pad = jnp.pad(kv_win, ((0, 0), (WINDOW, 0), (0, 0), (0, 0), (0, 0)))
    nb = T // Q_BLOCK
    qb = q.reshape(B, nb, Q_BLOCK, NSA_G, NSA_HPG, NSA_DH).swapaxes(0, 1)
    gb = gates.reshape(B, nb, Q_BLOCK, NSA_G, NSA_HPG, 3).swapaxes(0, 1)

    def block(args):
        q_i, g_i, i = args
        start = i * Q_BLOCK
        t = start + jnp.arange(Q_BLOCK)
        kvw = lax.dynamic_slice_in_dim(kvw_pad, start, WINDOW + Q_BLOCK, axis=1)
        s_win = start - WINDOW + jnp.arange(WINDOW + Q_BLOCK)
        return nsa_core(q_i, t, g_i, kvc, gather, n_sel, kvw, s_win)

    o = lax.map(block, (qb, gb, jnp.arange(nb)))
    return o.swapaxes(0, 1).reshape(B, T, NSA_W)


def nsa_sample(q, kv_cmp, kv_slc, kv_win, gates, pool_cmp, pool_slc, win_buf, page_table, lp):
    DB, Tn = q.shape[:2]
    past = page_table.shape[1] * PAGE_SIZE
    cmp_rows = pool_cmp[page_table].reshape(DB, past, 2, NSA_G, NSA_DH)
    cmp_rows = jnp.concatenate([cmp_rows, kv_cmp.astype(cmp_rows.dtype)], axis=1)
    kvc = nsa_compress(cmp_rows, lp['nsa_cmp_pe'], lp['nsa_cmp_w1'], lp['nsa_cmp_w2'])
    gather, n_sel = make_paged_gather(pool_slc, page_table, kv_slc)
    l_buf = win_buf.shape[1]
    kvw = jnp.concatenate([win_buf, kv_win.astype(win_buf.dtype)], axis=1)
    s_win = past - l_buf + jnp.arange(l_buf + Tn)
    t = past + jnp.arange(Tn)
    o = nsa_core(q, t, gates, kvc, gather, n_sel, kvw, s_win)
    return o.reshape(DB, Tn, NSA_W), kvw[:, -min(WINDOW, l_buf + Tn):]


def merge_and_ffn(x, u, oa, ob, oc, lp):
    B, T, _ = x.shape
    dt = x.dtype
    pa = oa.astype(dt) @ lp['w_branch_a']
    pb = ob.astype(dt) @ lp['w_branch_b']
    pc = oc.astype(dt) @ lp['w_branch_c']
    gm = jax.nn.sigmoid(u[..., OFF_MERGE:OFF_MERGE + 3 * D_MODEL]).reshape(B, T, 3, D_MODEL)
    mix = (gm[:, :, 0] * pa + gm[:, :, 1] * pb + gm[:, :, 2] * pc) @ lp['w_out']
    x = layer_norm(DEEPNORM_ALPHA * x + mix, lp['ln1_g'], lp['ln1_b'])
    a, b = jnp.split(x @ lp['w_ffn_in'], 2, axis=-1)
    y = (jax.nn.silu(a) * b) @ lp['w_ffn_out']
    return layer_norm(DEEPNORM_ALPHA * x + y, lp['ln2_g'], lp['ln2_b'])


def layer_prompt(x, lp):
    T = x.shape[1]
    u = x @ lp['w_in']
    qkv_p = jnp.pad(u[..., OFF_DN_QKV:OFF_DN_QKV + DN_CONV_C], ((0, 0), (DN_CONV - 1, 0), (0, 0)))
    q, k, v, beta, g = dn_features(causal_dwconv(qkv_p, lp['dn_conv_w']), u[..., OFF_DN_B:OFF_DN_B + DN_H],
                                   u[..., OFF_DN_A:OFF_DN_A + DN_H], lp['dn_a_log'], lp['dn_dt_bias'])
    o, S = gated_delta_chunked(q, k, v, beta, g)
    oa = dn_output(o, u[..., OFF_DN_Z:OFF_DN_Z + DN_W], lp['dn_norm_g'])
    glu_p = jnp.pad(cf_glu(u[..., OFF_CF:OFF_CF + 2 * CF_C]), ((0, 0), (CF_K - 1, 0), (0, 0)))
    ob = cf_tail(glu_p, lp)
    nq, kv_cmp, kv_slc, kv_win, ng = nsa_split(u)
    oc = nsa_prompt(nq, kv_cmp, kv_slc, kv_win, ng, lp)
    y = merge_and_ffn(x, u, oa, ob, oc, lp)
    states = (kv_cmp, kv_slc, kv_win[:, -min(WINDOW, T):], S.astype(x.dtype),
              qkv_p[:, -(DN_CONV - 1):], glu_p[:, -(CF_K - 1):])
    return y, states


def layer_sample(x, pool_cmp, pool_slc, win_buf, dn_state, dn_conv_state, cf_conv_state, page_table, lp):
    u = x @ lp['w_in']
    qkv_p = jnp.concatenate([dn_conv_state.astype(u.dtype), u[..., OFF_DN_QKV:OFF_DN_QKV + DN_CONV_C]], axis=1)
    q, k, v, beta, g = dn_features(causal_dwconv(qkv_p, lp['dn_conv_w']), u[..., OFF_DN_B:OFF_DN_B + DN_H],
                                   u[..., OFF_DN_A:OFF_DN_A + DN_H], lp['dn_a_log'], lp['dn_dt_bias'])
    o, S = gated_delta_recurrent(dn_state.astype(F32), q, k, v, beta, g)
    oa = dn_output(o, u[..., OFF_DN_Z:OFF_DN_Z + DN_W], lp['dn_norm_g'])
    glu_p = jnp.concatenate([cf_conv_state.astype(u.dtype), cf_glu(u[..., OFF_CF:OFF_CF + 2 * CF_C])], axis=1)
    ob = cf_tail(glu_p, lp)
    nq, kv_cmp, kv_slc, kv_win, ng = nsa_split(u)
    oc, new_win = nsa_sample(nq, kv_cmp, kv_slc, kv_win, ng, pool_cmp, pool_slc, win_buf, page_table, lp)
    y = merge_and_ffn(x, u, oa, ob, oc, lp)
    states = (kv_cmp, kv_slc, new_win, S.astype(dn_state.dtype),
              qkv_p[:, -(DN_CONV - 1):], glu_p[:, -(CF_K - 1):])
    return y, states


def setup_inputs(seed: int = 0) -> dict:
    key = jax.random.key(seed)
    ks = jax.random.split(key, 32)
    n_pages = PAST_LEN // PAGE_SIZE
    n_used = DEC_BATCH * n_pages
    n_pool = (5 * n_used + 3) // 4
    win_buf = min(WINDOW, PAST_LEN)

    def nrm(i, shape, scale=1.0):
        return jax.random.normal(ks[i], shape, F32) * scale

    dt = jnp.exp(jax.random.uniform(ks[12], (DEPTH, DN_H), F32, math.log(1e-3), math.log(0.1)))
    return {
        'x_prompt': nrm(0, (BATCH, SEQ, D_MODEL)),
        'x_sample': nrm(1, (DEC_BATCH, DEC_SEQ, D_MODEL)),
        'cache_nsa_cmp': nrm(2, (DEPTH, n_pool, PAGE_SIZE, 2, NSA_G, NSA_DH)),
        'cache_nsa_slc': nrm(3, (DEPTH, n_pool, PAGE_SIZE, 2, NSA_G, NSA_DH)),
        'cache_nsa_win': nrm(4, (DEPTH, DEC_BATCH, win_buf, 2, NSA_G, NSA_DH)),
        'state_dn': nrm(5, (DEPTH, DEC_BATCH, DN_H, DN_DK, DN_DV), 0.1),
        'state_dn_conv': nrm(6, (DEPTH, DEC_BATCH, DN_CONV - 1, DN_CONV_C)),
        'state_cf_conv': nrm(7, (DEPTH, DEC_BATCH, CF_K - 1, CF_C), 0.5),
        'page_table': jax.random.permutation(ks[8], n_pool)[:n_used].reshape(DEC_BATCH, n_pages).astype(jnp.int32),
        'w_in': nrm(9, (DEPTH, D_MODEL, N_IN), D_MODEL ** -0.5),
        'dn_conv_w': nrm(10, (DEPTH, DN_CONV, DN_CONV_C), DN_CONV ** -0.5),
        'dn_a_log': jnp.log(jax.random.uniform(ks[11], (DEPTH, DN_H), F32, 1.0, 16.0)),
        'dn_dt_bias': dt + jnp.log(-jnp.expm1(-dt)),
        'dn_norm_g': 1.0 + nrm(13, (DEPTH, DN_DV), 0.02),
        'cf_dw_w': nrm(14, (DEPTH, CF_K, CF_C), CF_K ** -0.5),
        'cf_dw_b': nrm(15, (DEPTH, CF_C), 0.02),
        'cf_ln_g': 1.0 + nrm(16, (DEPTH, CF_C), 0.02),
        'cf_ln_b': nrm(17, (DEPTH, CF_C), 0.02),
        'nsa_cmp_pe': nrm(18, (DEPTH, 2, L_CMP, NSA_DH), 0.02),
        'nsa_cmp_w1': nrm(19, (DEPTH, 2, L_CMP * NSA_DH, NSA_DH), (L_CMP * NSA_DH) ** -0.5),
        'nsa_cmp_w2': nrm(20, (DEPTH, 2, NSA_DH, NSA_DH), NSA_DH ** -0.5),
        'w_branch_a': nrm(21, (DEPTH, DN_W, D_MODEL), DEEPNORM_BETA * DN_W ** -0.5),
        'w_branch_b': nrm(22, (DEPTH, CF_C, D_MODEL), DEEPNORM_BETA * CF_C ** -0.5),
        'w_branch_c': nrm(23, (DEPTH, NSA_W, D_MODEL), DEEPNORM_BETA * NSA_W ** -0.5),
        'w_out': nrm(24, (DEPTH, D_MODEL, D_MODEL), DEEPNORM_BETA * D_MODEL ** -0.5),
        'ln1_g': 1.0 + nrm(25, (DEPTH, D_MODEL), 0.02),
        'ln1_b': nrm(26, (DEPTH, D_MODEL), 0.02),
        'ln2_g': 1.0 + nrm(27, (DEPTH, D_MODEL), 0.02),
        'ln2_b': nrm(28, (DEPTH, D_MODEL), 0.02),
        'w_ffn_in': nrm(29, (DEPTH, D_MODEL, 2 * D_FF), DEEPNORM_BETA * D_MODEL ** -0.5),
        'w_ffn_out': nrm(30, (DEPTH, D_FF, D_MODEL), DEEPNORM_BETA * D_FF ** -0.5),
    }


def reference(x_prompt, x_sample, cache_nsa_cmp, cache_nsa_slc, cache_nsa_win, state_dn, state_dn_conv,
              state_cf_conv, page_table, w_in, dn_conv_w, dn_a_log, dn_dt_bias, dn_norm_g, cf_dw_w, cf_dw_b,
              cf_ln_g, cf_ln_b, nsa_cmp_pe, nsa_cmp_w1, nsa_cmp_w2, w_branch_a, w_branch_b, w_branch_c, w_out,
              ln1_g, ln1_b, ln2_g, ln2_b, w_ffn_in, w_ffn_out):
    weights = {'w_in': w_in, 'dn_conv_w': dn_conv_w, 'dn_a_log': dn_a_log, 'dn_dt_bias': dn_dt_bias,
               'dn_norm_g': dn_norm_g, 'cf_dw_w': cf_dw_w, 'cf_dw_b': cf_dw_b, 'cf_ln_g': cf_ln_g,
               'cf_ln_b': cf_ln_b, 'nsa_cmp_pe': nsa_cmp_pe, 'nsa_cmp_w1': nsa_cmp_w1, 'nsa_cmp_w2': nsa_cmp_w2,
               'w_branch_a': w_branch_a, 'w_branch_b': w_branch_b, 'w_branch_c': w_branch_c, 'w_out': w_out,
               'ln1_g': ln1_g, 'ln1_b': ln1_b, 'ln2_g': ln2_g, 'ln2_b': ln2_b,
               'w_ffn_in': w_ffn_in, 'w_ffn_out': w_ffn_out}
    xp, xs = x_prompt, x_sample
    st_p, st_s = [], []
    for l in range(DEPTH):
        lp = {name: w[l] for name, w in weights.items()}
        xp, sp = layer_prompt(xp, lp)
        xs, ss = layer_sample(xs, cache_nsa_cmp[l], cache_nsa_slc[l], cache_nsa_win[l], state_dn[l],
                              state_dn_conv[l], state_cf_conv[l], page_table, lp)
        st_p.append(sp)
        st_s.append(ss)
    cmp_p = jnp.stack([s[0] for s in st_p])
    cmp_s = jnp.stack([s[0] for s in st_s])
    slc_p = jnp.stack([s[1] for s in st_p])
    slc_s = jnp.stack([s[1] for s in st_s])
    win_p = jnp.stack([s[2] for s in st_p])
    win_s = jnp.stack([s[2] for s in st_s])
    dn_p = jnp.stack([s[3] for s in st_p])
    dn_s = jnp.stack([s[3] for s in st_s])
    dnc_p = jnp.stack([s[4] for s in st_p])
    dnc_s = jnp.stack([s[4] for s in st_s])
    cfc_p = jnp.stack([s[5] for s in st_p])
    cfc_s = jnp.stack([s[5] for s in st_s])
    return (xp, xs, cmp_p, cmp_s, slc_p, slc_s, win_p, win_s, dn_p, dn_s, dnc_p, dnc_s, cfc_p, cfc_s)
```

```python
import functools

import jax
import jax.numpy as jnp
from jax import lax
from jax.experimental import pallas as pl
from jax.experimental.pallas import tpu as pltpu

F32 = jnp.float32
BF = jnp.bfloat16
HI = lax.Precision.HIGHEST

D_MODEL = 1024
DN_H = 4
DN_DK = 128
DN_DV = 128
DN_QK = DN_H * DN_DK
DN_W = DN_H * DN_DV
DN_CONV_C = 2 * DN_QK + DN_W
DN_CONV = 4
DN_CHUNK = 64
CF_C = 512
CF_K = 31
NSA_H = 8
NSA_G = 2
NSA_HPG = NSA_H // NSA_G
NSA_DH = 64
NSA_W = NSA_H * NSA_DH
NSA_KV = 3 * 2 * NSA_G * NSA_DH
NSA_ROW = 2 * NSA_G * NSA_DH
L_CMP = 32
L_SEL = 64
N_TOP = 16
WINDOW = 512
Q_BLOCK = 128
FORCE_SCORE = 1.0e4
PAGE_SIZE = 128
D_FF = 2816
FFN_CHUNKS = 2
NEG_BIG = -1.0e30

OFF_DN_Z = DN_CONV_C
OFF_DN_B = OFF_DN_Z + DN_W
OFF_DN_A = OFF_DN_B + DN_H
OFF_CF = OFF_DN_A + DN_H
OFF_NSA_Q = OFF_CF + 2 * CF_C
OFF_NSA_KV = OFF_NSA_Q + NSA_W
OFF_NSA_G = OFF_NSA_KV + NSA_KV
OFF_MERGE = OFF_NSA_G + 3 * NSA_H
N_IN = OFF_MERGE + 3 * D_MODEL

P_MERGE = 0
P_DNQKV = 3072
P_NKV = 4608
P_SMALL = 5376
P_DNZ = 5632
P_CF = 6144
P_NQ = 7168
N_PAD = 7680
SM_B = 0
SM_A = DN_H
SM_G = 2 * DN_H

DEEPNORM_ALPHA = (2 * 2) ** 0.25


def _alpha(depth):
    return (2 * depth) ** 0.25


def _dot(a, b, precision=None):
    return jnp.dot(a, b, preferred_element_type=F32, precision=precision)


def _dot_nt(a, b):
    return lax.dot_general(a, b, (((1,), (1,)), ((), ())), preferred_element_type=F32)


def _dot_tn(a, b):
    return lax.dot_general(a, b, (((0,), (0,)), ((), ())), preferred_element_type=F32)


def _silu(x):
    return x * jax.nn.sigmoid(x)


def _softplus(x):
    return jnp.maximum(x, 0.0) + jnp.log(1.0 + jnp.exp(-jnp.abs(x)))


def _l2norm(x):
    return x * lax.rsqrt(jnp.sum(x * x, -1, keepdims=True) + 1e-6)


def _layer_norm(x, g, b):
    xc = x - jnp.mean(x, -1, keepdims=True)
    var = jnp.mean(xc * xc, -1, keepdims=True)
    return xc * lax.rsqrt(var + 1e-5) * g + b


def _masked_softmax(s, mask):
    s = jnp.where(mask, s, -jnp.inf)
    m = jnp.max(s, -1, keepdims=True)
    m = jnp.where(m > -jnp.inf, m, 0.0)
    e = jnp.exp(s - m)
    den = jnp.sum(e, -1, keepdims=True)
    return e / jnp.where(den > 0, den, 1.0)


def _params(*sem, vmem_mb=None):
    kw = {}
    if vmem_mb is not None:
        kw["vmem_limit_bytes"] = vmem_mb * 1024 * 1024
    return pltpu.CompilerParams(dimension_semantics=sem, **kw)


def _mm_kernel(x_ref, w_ref, o_ref):
    o_ref[...] = _dot(x_ref[...].astype(BF), w_ref[...])


def _matmul(x, w, tm, tn):
    m, k = x.shape
    n = w.shape[1]
    return pl.pallas_call(
        _mm_kernel,
        grid=(m // tm, n // tn),
        in_specs=[pl.BlockSpec((tm, k), lambda i, j: (i, 0)),
                  pl.BlockSpec((k, tn), lambda i, j: (0, j))],
        out_specs=pl.BlockSpec((tm, tn), lambda i, j: (i, j)),
        out_shape=jax.ShapeDtypeStruct((m, n), F32),
        compiler_params=_params("arbitrary", "arbitrary"),
        name="proj_matmul",
    )(x, w)


def _dn_gates(sm, par):
    beta = jax.nn.sigmoid(sm)
    g = -jnp.exp(par[0:1, :]) * _softplus(sm + par[1:2, :])
    return beta, g


def _dn_prompt_kernel(qkv_ref, z_ref, sm_ref, cw_ref, par_ref, ng_ref, oa_ref, st_ref, xp_ref, s_ref, *, n_chunks):
    n = pl.program_id(1)
    c = DN_CHUNK

    @pl.when(n == 0)
    def _():
        xp_ref[0:8, :] = jnp.zeros((8, DN_CONV_C), F32)
        s_ref[...] = jnp.zeros_like(s_ref)

    raw = qkv_ref[...]
    xp_ref[8:8 + c, :] = raw
    cw = cw_ref[...]
    conv = cw[0:1, :] * xp_ref[5:5 + c, :]
    for k in range(1, DN_CONV):
        conv = conv + cw[k:k + 1, :] * xp_ref[5 + k:5 + k + c, :]
    xp_ref[0:8, :] = raw[c - 8:c, :]
    h = _silu(conv)

    beta_all, g_all = _dn_gates(sm_ref[...], par_ref[...])
    ii = lax.broadcasted_iota(jnp.int32, (c, c), 0)
    jj = lax.broadcasted_iota(jnp.int32, (c, c), 1)
    tri = jnp.where(ii >= jj, 1.0, 0.0).astype(F32)
    eye = jnp.where(ii == jj, 1.0, 0.0).astype(F32)
    gcum = _dot(tri, g_all, precision=HI)
    gcum_t = gcum.T
    z = z_ref[...]
    ng = ng_ref[...]

    outs = []
    for hd in range(DN_H):
        qh = _l2norm(h[:, hd * DN_DK:(hd + 1) * DN_DK]) * (DN_DK ** -0.5)
        kh = _l2norm(h[:, DN_QK + hd * DN_DK:DN_QK + (hd + 1) * DN_DK])
        vh = h[:, 2 * DN_QK + hd * DN_DV:2 * DN_QK + (hd + 1) * DN_DV]
        bcol = beta_all[:, SM_B + hd:SM_B + hd + 1]
        gc = gcum[:, SM_A + hd:SM_A + hd + 1]
        gr = gcum_t[SM_A + hd:SM_A + hd + 1, :]
        diff = gc - gr
        dec_incl = jnp.exp(jnp.where(ii >= jj, diff, -jnp.inf))
        dec_strict = jnp.where(ii > jj, dec_incl, 0.0)
        kb = kh.astype(BF)
        kk = _dot_nt(kb, kb)
        nm = -(bcol * kk * dec_strict)
        inv = eye + nm
        pw = nm
        for _ in range(5):
            pw = _dot(pw, pw, precision=HI)
            inv = inv + _dot(inv, pw, precision=HI)
        eg = jnp.exp(gc)
        rhs = jnp.concatenate([bcol * vh, (bcol * eg) * kh], axis=1)
        xsol = _dot(inv, rhs, precision=HI)
        w_val = xsol[:, :DN_DV]
        k_cum = xsol[:, DN_DV:]
        qk = _dot_nt(qh.astype(BF), kb) * dec_incl
        q_dec = qh * eg
        g_last = gc[c - 1:c, :]
        k_dec = kh * jnp.exp(g_last - gc)
        s_old = s_ref[hd]
        sb = s_old.astype(BF)
        u = w_val - _dot(k_cum.astype(BF), sb)
        ub = u.astype(BF)
        o = _dot(q_dec.astype(BF), sb) + _dot(qk.astype(BF), ub)
        s_ref[hd] = jnp.exp(g_last) * s_old + _dot_tn(k_dec.astype(BF), ub)
        o = o * lax.rsqrt(jnp.mean(o * o, -1, keepdims=True) + 1e-6) * ng
        outs.append(o * _silu(z[:, hd * DN_DV:(hd + 1) * DN_DV]))
    oa_ref[...] = jnp.concatenate(outs, axis=1).astype(BF)

    @pl.when(n == n_chunks - 1)
    def _():
        st_ref[0] = s_ref[...]


def _dn_prompt(u_p, conv_w, par, ng, b, t):
    nc = t // DN_CHUNK
    c = DN_CHUNK
    return pl.pallas_call(
        functools.partial(_dn_prompt_kernel, n_chunks=nc),
        grid=(b, nc),
        in_specs=[pl.BlockSpec((c, DN_CONV_C), lambda i, n: (i * nc + n, P_DNQKV // DN_CONV_C)),
                  pl.BlockSpec((c, DN_W), lambda i, n: (i * nc + n, P_DNZ // DN_W)),
                  pl.BlockSpec((c, 128), lambda i, n: (i * nc + n, P_SMALL // 128)),
                  pl.BlockSpec((DN_CONV, DN_CONV_C), lambda i, n: (0, 0)),
                  pl.BlockSpec((8, 128), lambda i, n: (0, 0)),
                  pl.BlockSpec((1, DN_DV), lambda i, n: (0, 0))],
        out_specs=[pl.BlockSpec((c, DN_W), lambda i, n: (i * nc + n, 0)),
                   pl.BlockSpec((1, DN_H, DN_DK, DN_DV), lambda i, n: (i, 0, 0, 0))],
        out_shape=[jax.ShapeDtypeStruct((b * t, DN_W), BF),
                   jax.ShapeDtypeStruct((b, DN_H, DN_DK, DN_DV), F32)],
        scratch_shapes=[pltpu.VMEM((8 + c, DN_CONV_C), F32),
                        pltpu.VMEM((DN_H, DN_DK, DN_DV), F32)],
        compiler_params=_params("arbitrary", "arbitrary"),
        name="dn_prompt",
    )(u_p, u_p, u_p, conv_w, par, ng)


CF_TILE = 256
CF_ROWS = 32
CF_HALO = 32


def _cf_prompt_kernel(u_ref, w_ref, bias_ref, g_ref, b_ref, ob_ref, tail_ref, buf_ref, *, n_tiles):
    i = pl.program_id(1)
    tc = CF_TILE

    @pl.when(i == 0)
    def _():
        buf_ref[0:CF_HALO, :] = jnp.zeros((CF_HALO, CF_C), F32)

    u = u_ref[...]
    buf_ref[CF_HALO:CF_HALO + tc, :] = u[:, :CF_C] * jax.nn.sigmoid(u[:, CF_C:])
    bias = bias_ref[...]
    g = g_ref[...]
    b = b_ref[...]
    off = CF_HALO - (CF_K - 1)
    for r0 in range(0, tc, CF_ROWS):
        acc = bias + w_ref[0:1, :] * buf_ref[r0 + off:r0 + off + CF_ROWS, :]
        for k in range(1, CF_K):
            acc = acc + w_ref[k:k + 1, :] * buf_ref[r0 + off + k:r0 + off + k + CF_ROWS, :]
        ob_ref[r0:r0 + CF_ROWS, :] = _silu(_layer_norm(acc, g, b)).astype(BF)
    tail = buf_ref[tc:tc + CF_HALO, :]
    buf_ref[0:CF_HALO, :] = tail

    @pl.when(i == n_tiles - 1)
    def _():
        tail_ref[0] = tail


def _cf_prompt(u_p, w, bias, g, bb, b, t):
    nt = t // CF_TILE
    return pl.pallas_call(
        functools.partial(_cf_prompt_kernel, n_tiles=nt),
        grid=(b, nt),
        in_specs=[pl.BlockSpec((CF_TILE, 2 * CF_C), lambda i, n: (i * nt + n, P_CF // (2 * CF_C))),
                  pl.BlockSpec((CF_K, CF_C), lambda i, n: (0, 0)),
                  pl.BlockSpec((1, CF_C), lambda i, n: (0, 0)),
                  pl.BlockSpec((1, CF_C), lambda i, n: (0, 0)),
                  pl.BlockSpec((1, CF_C), lambda i, n: (0, 0))],
        out_specs=[pl.BlockSpec((CF_TILE, CF_C), lambda i, n: (i * nt + n, 0)),
                   pl.BlockSpec((1, CF_HALO, CF_C), lambda i, n: (i, 0, 0))],
        out_shape=[jax.ShapeDtypeStruct((b * t, CF_C), BF),
                   jax.ShapeDtypeStruct((b, CF_HALO, CF_C), F32)],
        scratch_shapes=[pltpu.VMEM((CF_HALO + CF_TILE, CF_C), F32)],
        compiler_params=_params("arbitrary", "arbitrary"),
        name="cf_prompt",
    )(u_p, w, bias, g, bb)


CMP_FLAT = L_CMP * NSA_ROW


def _cmp_mlp(x, pe, w1, w2):
    xb = (x + pe).astype(BF)
    h = _silu(_dot(xb, w1))
    return _dot(h.astype(BF), w2)


def _cmp_kernel(x_ref, pe_ref, w1_ref, w2_ref, o_ref):
    o_ref[...] = _cmp_mlp(x_ref[...], pe_ref[...], w1_ref[...], w2_ref[...])


def _compress_prompt(x, pe, w1, w2, tr):
    r = x.shape[0]
    return pl.pallas_call(
        _cmp_kernel,
        grid=(r // tr,),
        in_specs=[pl.BlockSpec((tr, CMP_FLAT), lambda i: (i, 0)),
                  pl.BlockSpec((1, CMP_FLAT), lambda i: (0, 0)),
                  pl.BlockSpec((CMP_FLAT, NSA_ROW), lambda i: (0, 0)),
                  pl.BlockSpec((NSA_ROW, NSA_ROW), lambda i: (0, 0))],
        out_specs=pl.BlockSpec((tr, NSA_ROW), lambda i: (i, 0)),
        out_shape=jax.ShapeDtypeStruct((r, NSA_ROW), F32),
        compiler_params=_params("arbitrary"),
        name="nsa_compress_prompt",
    )(x, pe, w1, w2)


CMP_PAGES = 16
BLK_PER_PAGE = PAGE_SIZE // L_CMP


def _cmp_sample_kernel(pt_ref, *refs):
    pages = refs[:CMP_PAGES]
    pe_ref, w1_ref, w2_ref, o_ref, xs_ref = refs[CMP_PAGES:]
    for i in range(CMP_PAGES):
        xs_ref[BLK_PER_PAGE * i:BLK_PER_PAGE * (i + 1), :] = pages[i][0, 0]
    o_ref[0] = _cmp_mlp(xs_ref[...], pe_ref[...], w1_ref[...], w2_ref[...])


def _compress_sample(cache4, pt_flat, layer, pe, w1, w2, db, n_pages):
    steps = n_pages // CMP_PAGES
    rows = CMP_PAGES * BLK_PER_PAGE

    def page_map(i):
        return lambda b, j, pt: (layer, pt[b * n_pages + j * CMP_PAGES + i], 0, 0)

    grid_spec = pltpu.PrefetchScalarGridSpec(
        num_scalar_prefetch=1,
        grid=(db, steps),
        in_specs=[pl.BlockSpec((1, 1, BLK_PER_PAGE, CMP_FLAT), page_map(i)) for i in range(CMP_PAGES)]
        + [pl.BlockSpec((1, CMP_FLAT), lambda b, j, pt: (0, 0)),
           pl.BlockSpec((CMP_FLAT, NSA_ROW), lambda b, j, pt: (0, 0)),
           pl.BlockSpec((NSA_ROW, NSA_ROW), lambda b, j, pt: (0, 0))],
        out_specs=pl.BlockSpec((1, rows, NSA_ROW), lambda b, j, pt: (b, j, 0)),
        scratch_shapes=[pltpu.VMEM((rows, CMP_FLAT), F32)],
    )
    return pl.pallas_call(
        _cmp_sample_kernel,
        grid_spec=grid_spec,
        out_shape=jax.ShapeDtypeStruct((db, n_pages * BLK_PER_PAGE, NSA_ROW), F32),
        compiler_params=_params("arbitrary", "arbitrary"),
        name="nsa_compress_sample",
    )(pt_flat, *([cache4] * CMP_PAGES), pe, w1, w2)


SEL_TK = 512


def _pair_matrix(nc, n_sel_pad):
    ci = lax.broadcasted_iota(jnp.int32, (nc, n_sel_pad), 0)
    si = lax.broadcasted_iota(jnp.int32, (nc, n_sel_pad), 1)
    return jnp.where((ci >> 1) == si, 1.0, 0.0).astype(F32)


def _head_gate(gates, g, hp, br):
    col = SM_G + (g * NSA_HPG + hp) * 3 + br
    return gates[:, col:col + 1]


def _nsa_prompt_kernel(q_ref, sm_ref, kvc_ref, ks_ref, kw_ref, o_ref, *, t_len):
    i = pl.program_id(1)
    qb = Q_BLOCK
    rows = NSA_HPG * qb
    start = i * qb
    nc = t_len // L_CMP
    n_sel = t_len // L_SEL
    q = q_ref[...] * (NSA_DH ** -0.5)
    gates = jax.nn.sigmoid(sm_ref[...])
    kvc = kvc_ref[0]
    t1 = start + lax.broadcasted_iota(jnp.int32, (qb, 1), 0)
    t4 = start + (lax.broadcasted_iota(jnp.int32, (rows, 1), 0) & (qb - 1))
    pair = _pair_matrix(nc, n_sel)
    blk = lax.broadcasted_iota(jnp.int32, (1, n_sel), 1)
    cur = t1 >> 6
    valid = blk * L_SEL <= t1
    forced = (blk == 0) | (blk == cur) | (blk == cur - 1)
    c_end = (lax.broadcasted_iota(jnp.int32, (1, nc), 1) + 1) * L_CMP - 1
    n_kt = (start + qb + SEL_TK - 1) // SEL_TK
    kstart = pl.multiple_of(jnp.maximum(start - WINDOW, 0), qb)
    wlen = WINDOW + qb
    spos = kstart + lax.broadcasted_iota(jnp.int32, (1, wlen), 1)
    dist = t4 - spos
    m_w = (dist >= 0) & (dist <= WINDOW)

    outs = []
    for g in range(NSA_G):
        q4 = jnp.concatenate(
            [q[:, (g * NSA_HPG + hp) * NSA_DH:(g * NSA_HPG + hp + 1) * NSA_DH] for hp in range(NSA_HPG)],
            axis=0).astype(BF)
        kcol = slice(g * NSA_DH, (g + 1) * NSA_DH)
        vcol = slice(NSA_G * NSA_DH + g * NSA_DH, NSA_G * NSA_DH + (g + 1) * NSA_DH)
        s_c = _dot_nt(q4, kvc[:, kcol].astype(BF))
        p_c = _masked_softmax(s_c, c_end <= t4)
        o_cmp = _dot(p_c.astype(BF), kvc[:, vcol].astype(BF))
        imp_c = p_c[0:qb] + p_c[qb:2 * qb] + p_c[2 * qb:3 * qb] + p_c[3 * qb:4 * qb]
        imp = _dot(imp_c, pair, precision=HI)
        score = jnp.where(valid, jnp.where(forced, FORCE_SCORE, imp), -jnp.inf)
        sel = jnp.zeros((qb, n_sel), F32)
        for _ in range(min(N_TOP, n_sel)):
            m = jnp.max(score, -1, keepdims=True)
            first = jnp.min(jnp.where(score == m, blk, n_sel), -1, keepdims=True)
            onehot = blk == first
            sel = jnp.where(onehot & (m > -jnp.inf), 1.0, sel)
            score = jnp.where(onehot, -jnp.inf, score)
        selb = sel.astype(BF)

        def body(j, carry):
            m_i, l_i, acc = carry
            k0 = pl.multiple_of(j * SEL_TK, SEL_TK)
            kt = ks_ref[0, pl.ds(k0, SEL_TK), kcol]
            vt = ks_ref[0, pl.ds(k0, SEL_TK), vcol]
            s = _dot_nt(q4, kt)
            kpos = k0 + lax.broadcasted_iota(jnp.int32, (1, SEL_TK), 1)
            expand = jnp.where(lax.broadcasted_iota(jnp.int32, (n_sel, SEL_TK), 0)
                               == ((k0 + lax.broadcasted_iota(jnp.int32, (n_sel, SEL_TK), 1)) >> 6), 1.0, 0.0)
            mexp = _dot(selb, expand.astype(BF))
            mf = jnp.where(kpos <= t1, mexp, 0.0)
            mask = jnp.concatenate([mf] * NSA_HPG, axis=0) > 0.5
            s = jnp.where(mask, s, NEG_BIG)
            m_new = jnp.maximum(m_i, jnp.max(s, -1, keepdims=True))
            alpha = jnp.exp(m_i - m_new)
            p = jnp.where(mask, jnp.exp(s - m_new), 0.0)
            l_new = alpha * l_i + jnp.sum(p, -1, keepdims=True)
            acc = alpha * acc + _dot(p.astype(BF), vt)
            return m_new, l_new, acc

        m_i, l_i, acc = lax.fori_loop(
            0, n_kt, body,
            (jnp.full((rows, 1), NEG_BIG, F32), jnp.zeros((rows, 1), F32), jnp.zeros((rows, NSA_DH), F32)))
        o_slc = acc / jnp.where(l_i > 0, l_i, 1.0)
        kw = kw_ref[0, pl.ds(kstart, wlen), kcol]
        vw = kw_ref[0, pl.ds(kstart, wlen), vcol]
        p_w = _masked_softmax(_dot_nt(q4, kw), m_w)
        o_win = _dot(p_w.astype(BF), vw)
        for hp in range(NSA_HPG):
            r = slice(hp * qb, (hp + 1) * qb)
            outs.append(_head_gate(gates, g, hp, 0) * o_cmp[r] + _head_gate(gates, g, hp, 1) * o_slc[r]
                        + _head_gate(gates, g, hp, 2) * o_win[r])
    o_ref[...] = jnp.concatenate(outs, axis=1).astype(BF)


def _nsa_prompt(u_p, kvc, kv_bf, b, t):
    nq = t // Q_BLOCK
    nc = t // L_CMP
    return pl.pallas_call(
        functools.partial(_nsa_prompt_kernel, t_len=t),
        grid=(b, nq),
        in_specs=[pl.BlockSpec((Q_BLOCK, NSA_W), lambda i, n: (i * nq + n, P_NQ // NSA_W)),
                  pl.BlockSpec((Q_BLOCK, 128), lambda i, n: (i * nq + n, P_SMALL // 128)),
                  pl.BlockSpec((1, nc, NSA_ROW), lambda i, n: (i, 0, 0)),
                  pl.BlockSpec((1, t, NSA_ROW), lambda i, n: (i, 0, 1)),
                  pl.BlockSpec((1, t, NSA_ROW), lambda i, n: (i, 0, 2))],
        out_specs=pl.BlockSpec((Q_BLOCK, NSA_W), lambda i, n: (i * nq + n, 0)),
        out_shape=jax.ShapeDtypeStruct((b * t, NSA_W), BF),
        compiler_params=_params("arbitrary", "arbitrary", vmem_mb=48),
        name="nsa_prompt",
    )(u_p, u_p, kvc, kv_bf, kv_bf)


def _merge_kernel(x_ref, oa_ref, ob_ref, oc_ref, gm_ref, pa_ref, pb_ref, pc_ref, wo_ref, g_ref, b_ref, o_ref, *, alpha):
    gm = jax.nn.sigmoid(gm_ref[...])
    d = D_MODEL
    mix_in = (gm[:, 0:d] * _dot(oa_ref[...], pa_ref[...]) + gm[:, d:2 * d] * _dot(ob_ref[...], pb_ref[...])
              + gm[:, 2 * d:3 * d] * _dot(oc_ref[...], pc_ref[...]))
    mix = _dot(mix_in.astype(BF), wo_ref[...])
    o_ref[...] = _layer_norm(alpha * x_ref[...] + mix, g_ref[...], b_ref[...])


def _merge(x, oa, ob, oc, u, pa, pb, pc, wo, g, b, tm, alpha):
    m = x.shape[0]
    row = lambda i: (i, 0)
    fix = lambda i: (0, 0)
    return pl.pallas_call(
        functools.partial(_merge_kernel, alpha=alpha),
        grid=(m // tm,),
        in_specs=[pl.BlockSpec((tm, D_MODEL), row),
                  pl.BlockSpec((tm, DN_W), row), pl.BlockSpec((tm, CF_C), row), pl.BlockSpec((tm, NSA_W), row),
                  pl.BlockSpec((tm, 3 * D_MODEL), lambda i: (i, P_MERGE // (3 * D_MODEL))),
                  pl.BlockSpec((DN_W, D_MODEL), fix), pl.BlockSpec((CF_C, D_MODEL), fix),
                  pl.BlockSpec((NSA_W, D_MODEL), fix), pl.BlockSpec((D_MODEL, D_MODEL), fix),
                  pl.BlockSpec((1, D_MODEL), fix), pl.BlockSpec((1, D_MODEL), fix)],
        out_specs=pl.BlockSpec((tm, D_MODEL), row),
        out_shape=jax.ShapeDtypeStruct((m, D_MODEL), F32),
        compiler_params=_params("arbitrary", vmem_mb=48),
        name="merge_ln",
    )(x, oa, ob, oc, u, pa, pb, pc, wo, g, b)


def _ffn_kernel(x_ref, wi_ref, wo_ref, g_ref, b_ref, o_ref, acc_ref, *, alpha):
    j = pl.program_id(1)
    fc = D_FF // FFN_CHUNKS
    x = x_ref[...]
    h = _dot(x.astype(BF), wi_ref[0])
    y = _dot((_silu(h[:, :fc]) * h[:, fc:]).astype(BF), wo_ref[0])

    @pl.when(j == 0)
    def _():
        acc_ref[...] = y

    @pl.when(j > 0)
    def _():
        acc_ref[...] += y

    @pl.when(j == FFN_CHUNKS - 1)
    def _():
        o_ref[...] = _layer_norm(alpha * x + acc_ref[...], g_ref[...], b_ref[...])


def _ffn(x, wi, wo, g, b, tm, alpha):
    m = x.shape[0]
    fc = D_FF // FFN_CHUNKS
    return pl.pallas_call(
        functools.partial(_ffn_kernel, alpha=alpha),
        grid=(m // tm, FFN_CHUNKS),
        in_specs=[pl.BlockSpec((tm, D_MODEL), lambda i, j: (i, 0)),
                  pl.BlockSpec((1, D_MODEL, 2 * fc), lambda i, j: (j, 0, 0)),
                  pl.BlockSpec((1, fc, D_MODEL), lambda i, j: (j, 0, 0)),
                  pl.BlockSpec((1, D_MODEL), lambda i, j: (0, 0)),
                  pl.BlockSpec((1, D_MODEL), lambda i, j: (0, 0))],
        out_specs=pl.BlockSpec((tm, D_MODEL), lambda i, j: (i, 0)),
        out_shape=jax.ShapeDtypeStruct((m, D_MODEL), F32),
        scratch_shapes=[pltpu.VMEM((tm, D_MODEL), F32)],
        compiler_params=_params("arbitrary", "arbitrary", vmem_mb=48),
        name="ffn_ln",
    )(x, wi, wo, g, b)


def _sample_mix_kernel(qkv_ref, z_ref, sm_ref, cf_ref, st_ref, dcs_ref, cfs_ref, cw_ref, par_ref, ng_ref,
                       fw_ref, fbias_ref, fg_ref, fb_ref, oa_ref, ob_ref, so_ref, glu_ref):
    cs = dcs_ref[0, 0]
    new = qkv_ref[0]
    cw = cw_ref[...]
    conv = cw[DN_CONV - 1:DN_CONV, :] * new
    for k in range(DN_CONV - 1):
        conv = conv + cw[k:k + 1, :] * cs[k:k + 1, :]
    h = _silu(conv)
    beta_all, g_all = _dn_gates(sm_ref[0], par_ref[...])
    z = z_ref[0]
    ng = ng_ref[...]
    outs = []
    for hd in range(DN_H):
        qh = _l2norm(h[:, hd * DN_DK:(hd + 1) * DN_DK]) * (DN_DK ** -0.5)
        kh = _l2norm(h[:, DN_QK + hd * DN_DK:DN_QK + (hd + 1) * DN_DK])
        vh = h[:, 2 * DN_QK + hd * DN_DV:2 * DN_QK + (hd + 1) * DN_DV]
        beta = beta_all[:, SM_B + hd:SM_B + hd + 1]
        g = g_all[:, SM_A + hd:SM_A + hd + 1]
        kcol = jnp.broadcast_to(kh, (DN_DK, DN_DK)).T
        qcol = jnp.broadcast_to(qh, (DN_DK, DN_DK)).T
        s = jnp.exp(g) * st_ref[0, 0, hd]
        u = beta * (vh - jnp.sum(s * kcol, 0, keepdims=True))
        s = s + kcol * u
        so_ref[0, hd] = s
        o = jnp.sum(s * qcol, 0, keepdims=True)
        o = o * lax.rsqrt(jnp.mean(o * o, -1, keepdims=True) + 1e-6) * ng
        outs.append(o * _silu(z[:, hd * DN_DV:(hd + 1) * DN_DV]))
    oa_ref[0] = jnp.concatenate(outs, axis=1).astype(BF)

    ucf = cf_ref[0]
    glu = ucf[:, :CF_C] * jax.nn.sigmoid(ucf[:, CF_C:])
    glu_ref[0] = glu
    fw = fw_ref[...]
    acc = fbias_ref[...] + jnp.sum(fw[0:CF_K - 1, :] * cfs_ref[0, 0], 0, keepdims=True) + fw[CF_K - 1:CF_K, :] * glu
    ob_ref[0] = _silu(_layer_norm(acc, fg_ref[...], fb_ref[...])).astype(BF)


def _sample_mix(u_s3, state_dn, state_dn_conv, state_cf_conv, layer, conv_w, par, ng, fw, fbias, fg, fb, db):
    fix = lambda b: (0, 0)
    return pl.pallas_call(
        _sample_mix_kernel,
        grid=(db,),
        in_specs=[pl.BlockSpec((1, 1, DN_CONV_C), lambda b: (b, 0, P_DNQKV // DN_CONV_C)),
                  pl.BlockSpec((1, 1, DN_W), lambda b: (b, 0, P_DNZ // DN_W)),
                  pl.BlockSpec((1, 1, 128), lambda b: (b, 0, P_SMALL // 128)),
                  pl.BlockSpec((1, 1, 2 * CF_C), lambda b: (b, 0, P_CF // (2 * CF_C))),
                  pl.BlockSpec((1, 1, DN_H, DN_DK, DN_DV), lambda b: (layer, b, 0, 0, 0)),
                  pl.BlockSpec((1, 1, DN_CONV - 1, DN_CONV_C), lambda b: (layer, b, 0, 0)),
                  pl.BlockSpec((1, 1, CF_K - 1, CF_C), lambda b: (layer, b, 0, 0)),
                  pl.BlockSpec((DN_CONV, DN_CONV_C), fix),
                  pl.BlockSpec((8, 128), fix),
                  pl.BlockSpec((1, DN_DV), fix),
                  pl.BlockSpec((CF_K, CF_C), fix),
                  pl.BlockSpec((1, CF_C), fix), pl.BlockSpec((1, CF_C), fix), pl.BlockSpec((1, CF_C), fix)],
        out_specs=[pl.BlockSpec((1, 1, DN_W), lambda b: (b, 0, 0)),
                   pl.BlockSpec((1, 1, CF_C), lambda b: (b, 0, 0)),
                   pl.BlockSpec((1, DN_H, DN_DK, DN_DV), lambda b: (b, 0, 0, 0)),
                   pl.BlockSpec((1, 1, CF_C), lambda b: (b, 0, 0))],
        out_shape=[jax.ShapeDtypeStruct((db, 1, DN_W), BF),
                   jax.ShapeDtypeStruct((db, 1, CF_C), BF),
                   jax.ShapeDtypeStruct((db, DN_H, DN_DK, DN_DV), F32),
                   jax.ShapeDtypeStruct((db, 1, CF_C), F32)],
        compiler_params=_params("arbitrary"),
        name="sample_dn_cf",
    )(u_s3, u_s3, u_s3, u_s3, state_dn, state_dn_conv, state_cf_conv, conv_w, par, ng, fw, fbias, fg, fb)


def _sample_q4(q, g):
    rows = [q[:, (g * NSA_HPG + hp) * NSA_DH:(g * NSA_HPG + hp + 1) * NSA_DH] for hp in range(NSA_HPG)]
    return jnp.concatenate(rows + [jnp.zeros((8 - NSA_HPG, NSA_DH), F32)], axis=0)


def _sample_sel_kernel(q_ref, kvc_ref, ocmp_ref, idx_ref, *, t_pos, n_sel, n_sel_pad):
    q = q_ref[0] * (NSA_DH ** -0.5)
    kvc = kvc_ref[0]
    nc = kvc.shape[0]
    c_end = (lax.broadcasted_iota(jnp.int32, (1, nc), 1) + 1) * L_CMP - 1
    pair = _pair_matrix(nc, n_sel_pad)
    blk = lax.broadcasted_iota(jnp.int32, (1, n_sel_pad), 1)
    cur = t_pos // L_SEL
    valid = (blk < n_sel) & (blk * L_SEL <= t_pos)
    forced = (blk == 0) | (blk == cur) | (blk == cur - 1)
    lane = lax.broadcasted_iota(jnp.int32, (1, 128), 1)
    o_parts = []
    idx_rows = []
    for g in range(NSA_G):
        q4 = _sample_q4(q, g).astype(BF)
        kcol = slice(g * NSA_DH, (g + 1) * NSA_DH)
        vcol = slice(NSA_G * NSA_DH + g * NSA_DH, NSA_G * NSA_DH + (g + 1) * NSA_DH)
        s_c = _dot_nt(q4, kvc[:, kcol].astype(BF))
        p_c = _masked_softmax(s_c, c_end <= t_pos)
        o_cmp = _dot(p_c.astype(BF), kvc[:, vcol].astype(BF))
        imp_c = p_c[0:1] + p_c[1:2] + p_c[2:3] + p_c[3:4]
        imp = _dot(jnp.broadcast_to(imp_c, (8, nc)), pair, precision=HI)[0:1]
        score = jnp.where(valid, jnp.where(forced, FORCE_SCORE, imp), -jnp.inf)
        idx_vec = jnp.full((1, 128), -1, jnp.int32)
        for r in range(min(N_TOP, n_sel)):
            m = jnp.max(score, -1, keepdims=True)
            first = jnp.min(jnp.where(score == m, blk, n_sel_pad), -1, keepdims=True)
            idx_vec = jnp.where((lane == r) & (m > -jnp.inf), first, idx_vec)
            score = jnp.where(blk == first, -jnp.inf, score)
        idx_rows.append(idx_vec)
        o_parts += [o_cmp[hp:hp + 1, :] for hp in range(NSA_HPG)]
    ocmp_ref[0] = jnp.concatenate(o_parts, axis=1)
    idx_ref[0] = jnp.concatenate(idx_rows, axis=0)


def _sample_select(u_s3, kvc_s, db, t_pos, n_sel):
    nc = kvc_s.shape[1]
    n_sel_pad = -(-n_sel // 128) * 128
    return pl.pallas_call(
        functools.partial(_sample_sel_kernel, t_pos=t_pos, n_sel=n_sel, n_sel_pad=n_sel_pad),
        grid=(db,),
        in_specs=[pl.BlockSpec((1, 1, NSA_W), lambda b: (b, 0, P_NQ // NSA_W)),
                  pl.BlockSpec((1, nc, NSA_ROW), lambda b: (b, 0, 0))],
        out_specs=[pl.BlockSpec((1, 1, NSA_W), lambda b: (b, 0, 0)),
                   pl.BlockSpec((1, NSA_G, 128), lambda b: (b, 0, 0))],
        out_shape=[jax.ShapeDtypeStruct((db, 1, NSA_W), F32),
                   jax.ShapeDtypeStruct((db, NSA_G, 128), jnp.int32)],
        compiler_params=_params("arbitrary"),
        name="nsa_sample_select",
    )(u_s3, kvc_s)


def _sample_attn_kernel(idx_ref, pt_ref, *refs, t_pos, n_past_blk, l_buf):
    nblk = NSA_G * N_TOP
    blks = refs[:nblk]
    q_ref, sm_ref, nkv_ref, win_ref, ocmp_ref, o_ref = refs[nblk:]
    b = pl.program_id(0)
    q = q_ref[0] * (NSA_DH ** -0.5)
    gates = jax.nn.sigmoid(sm_ref[0])
    newkv = nkv_ref[0]
    ocmp = ocmp_ref[0]
    nkeys = N_TOP * L_SEL
    lane = lax.broadcasted_iota(jnp.int32, (1, nkeys), 1)
    kslot = lane >> 6
    krow = lane & (L_SEL - 1)
    rowi = lax.broadcasted_iota(jnp.int32, (L_SEL, 1), 0)
    wpos = t_pos - l_buf + lax.broadcasted_iota(jnp.int32, (1, l_buf), 1)
    wdist = t_pos - wpos
    m_w = (wdist >= 0) & (wdist <= WINDOW) & (wpos >= 0)
    outs = []
    for g in range(NSA_G):
        q4f = _sample_q4(q, g)
        q4 = q4f.astype(BF)
        kcol = slice(g * NSA_DH, (g + 1) * NSA_DH)
        vcol = slice(NSA_G * NSA_DH + g * NSA_DH, NSA_G * NSA_DH + (g + 1) * NSA_DH)
        new_k = newkv[:, NSA_ROW + g * NSA_DH:NSA_ROW + (g + 1) * NSA_DH]
        new_v = newkv[:, NSA_ROW + NSA_G * NSA_DH + g * NSA_DH:NSA_ROW + NSA_G * NSA_DH + (g + 1) * NSA_DH]
        ks, vs = [], []
        idxv = jnp.zeros((1, nkeys), jnp.int32)
        for k in range(N_TOP):
            ik = idx_ref[(b * NSA_G + g) * N_TOP + k]
            blk = blks[g * N_TOP + k][0, 0]
            sub = (ik >= n_past_blk) & (rowi == 0)
            ks.append(jnp.where(sub, new_k, blk[:, kcol]))
            vs.append(jnp.where(sub, new_v, blk[:, vcol]))
            idxv = jnp.where(kslot == k, ik, idxv)
        kmat = jnp.concatenate(ks, axis=0).astype(BF)
        vmat = jnp.concatenate(vs, axis=0).astype(BF)
        pos = idxv * L_SEL + krow
        p_s = _masked_softmax(_dot_nt(q4, kmat), (idxv >= 0) & (pos <= t_pos))
        o_slc = _dot(p_s.astype(BF), vmat)
        wb = win_ref[0, 0]
        nwk = newkv[:, 2 * NSA_ROW + g * NSA_DH:2 * NSA_ROW + (g + 1) * NSA_DH]
        nwv = newkv[:, 2 * NSA_ROW + NSA_G * NSA_DH + g * NSA_DH:2 * NSA_ROW + NSA_G * NSA_DH + (g + 1) * NSA_DH]
        s_w = jnp.where(m_w, _dot_nt(q4, wb[:, kcol].astype(BF)), -jnp.inf)
        s_n = jnp.sum(q4.astype(F32) * nwk.astype(BF).astype(F32), -1, keepdims=True)
        m = jnp.maximum(jnp.max(s_w, -1, keepdims=True), s_n)
        e_w = jnp.exp(s_w - m)
        e_n = jnp.exp(s_n - m)
        den = jnp.sum(e_w, -1, keepdims=True) + e_n
        o_win = (_dot((e_w / den).astype(BF), wb[:, vcol].astype(BF))
                 + (e_n / den).astype(BF).astype(F32) * nwv.astype(BF).astype(F32))
        for hp in range(NSA_HPG):
            hcol = slice((g * NSA_HPG + hp) * NSA_DH, (g * NSA_HPG + hp + 1) * NSA_DH)
            outs.append(_head_gate(gates, g, hp, 0) * ocmp[:, hcol] + _head_gate(gates, g, hp, 1) * o_slc[hp:hp + 1]
                        + _head_gate(gates, g, hp, 2) * o_win[hp:hp + 1])
    o_ref[0] = jnp.concatenate(outs, axis=1).astype(BF)


def _sample_attn(idx_flat, pt_flat, slc_half, u_s3, win4, ocmp, layer, db, n_pages, t_pos, l_buf):
    n_past_blk = n_pages * (PAGE_SIZE // L_SEL)
    sub = PAGE_SIZE // L_SEL

    def blk_map(g, k):
        def f(b, idx, pt):
            pi = jnp.clip(idx[(b * NSA_G + g) * N_TOP + k], 0, n_past_blk - 1)
            return (layer, pt[b * n_pages + pi // sub] * sub + pi % sub, 0, 0)
        return f

    fix3 = lambda col: (lambda b, idx, pt: (b, 0, col))
    grid_spec = pltpu.PrefetchScalarGridSpec(
        num_scalar_prefetch=2,
        grid=(db,),
        in_specs=[pl.BlockSpec((1, 1, L_SEL, NSA_ROW), blk_map(g, k)) for g in range(NSA_G) for k in range(N_TOP)]
        + [pl.BlockSpec((1, 1, NSA_W), fix3(P_NQ // NSA_W)),
           pl.BlockSpec((1, 1, 128), fix3(P_SMALL // 128)),
           pl.BlockSpec((1, 1, NSA_KV), fix3(P_NKV // NSA_KV)),
           pl.BlockSpec((1, 1, l_buf, NSA_ROW), lambda b, idx, pt: (layer, b, 0, 0)),
           pl.BlockSpec((1, 1, NSA_W), fix3(0))],
        out_specs=pl.BlockSpec((1, 1, NSA_W), fix3(0)),
    )
    return pl.pallas_call(
        functools.partial(_sample_attn_kernel, t_pos=t_pos, n_past_blk=n_past_blk, l_buf=l_buf),
        grid_spec=grid_spec,
        out_shape=jax.ShapeDtypeStruct((db, 1, NSA_W), BF),
        compiler_params=_params("arbitrary"),
        name="nsa_sample_attn",
    )(idx_flat, pt_flat, *([slc_half] * (NSA_G * N_TOP)), u_s3, u_s3, u_s3, win4, ocmp)


def _pack_w_in(w):
    zeros = jnp.zeros((w.shape[0], 128 - 2 * DN_H - 3 * NSA_H + 128), w.dtype)
    return jnp.concatenate(
        [w[:, OFF_MERGE:N_IN], w[:, 0:DN_CONV_C], w[:, OFF_NSA_KV:OFF_NSA_G], w[:, OFF_DN_B:OFF_CF],
         w[:, OFF_NSA_G:OFF_MERGE], zeros, w[:, OFF_DN_Z:OFF_DN_B], w[:, OFF_CF:OFF_NSA_Q],
         w[:, OFF_NSA_Q:OFF_NSA_KV]], axis=1).astype(BF)


def _pack_cmp_weights(pe, w1, w2):
    eye = jnp.eye(2, dtype=F32)
    w1r = w1.reshape(2, L_CMP, NSA_DH, NSA_DH)
    w1big = jnp.einsum("slde,sp,gq->lsgdpqe", w1r, eye, eye).reshape(CMP_FLAT, NSA_ROW).astype(BF)
    w2big = jnp.einsum("sde,sp,gq->sgdpqe", w2, eye, eye).reshape(NSA_ROW, NSA_ROW).astype(BF)
    pe_flat = jnp.broadcast_to(pe.transpose(1, 0, 2)[:, :, None, :], (L_CMP, 2, NSA_G, NSA_DH)).reshape(1, CMP_FLAT)
    return pe_flat, w1big, w2big


def _pack_dn_par(a_log, dt_bias):
    par = jnp.zeros((8, 128), F32)
    par = par.at[0, SM_A:SM_A + DN_H].set(a_log)
    return par.at[1, SM_A:SM_A + DN_H].set(dt_bias)


def _pack_ffn(w_in, w_out):
    fc = D_FF // FFN_CHUNKS
    wi = jnp.stack([jnp.concatenate([w_in[:, c * fc:(c + 1) * fc], w_in[:, D_FF + c * fc:D_FF + (c + 1) * fc]], axis=1)
                    for c in range(FFN_CHUNKS)]).astype(BF)
    wo = w_out.reshape(FFN_CHUNKS, fc, D_MODEL).astype(BF)
    return wi, wo


def kernel(x_prompt, x_sample, cache_nsa_cmp, cache_nsa_slc, cache_nsa_win, state_dn, state_dn_conv, state_cf_conv, page_table, w_in, dn_conv_w, dn_a_log, dn_dt_bias, dn_norm_g, cf_dw_w, cf_dw_b, cf_ln_g, cf_ln_b, nsa_cmp_pe, nsa_cmp_w1, nsa_cmp_w2, w_branch_a, w_branch_b, w_branch_c, w_out, ln1_g, ln1_b, ln2_g, ln2_b, w_ffn_in, w_ffn_out):
    bsz, t, d = x_prompt.shape
    db = x_sample.shape[0]
    depth = w_in.shape[0]
    n_pool = cache_nsa_cmp.shape[1]
    n_pages = page_table.shape[1]
    past = n_pages * PAGE_SIZE
    l_buf = cache_nsa_win.shape[2]
    alpha = _alpha(depth)
    assert x_sample.shape[1] == 1 and d == D_MODEL
    assert t % CF_TILE == 0 and t >= WINDOW + Q_BLOCK and t % SEL_TK == 0 and n_pages % CMP_PAGES == 0

    cmp4 = cache_nsa_cmp.reshape(depth, n_pool, BLK_PER_PAGE, CMP_FLAT)
    slc_half = cache_nsa_slc.reshape(depth, n_pool * (PAGE_SIZE // L_SEL), L_SEL, NSA_ROW)
    win4 = cache_nsa_win.reshape(depth, db, l_buf, NSA_ROW)
    pt_flat = page_table.reshape(-1)
    tm_p = 512 if (bsz * t) % 512 == 0 else 256

    xp = x_prompt.reshape(bsz * t, d)
    xs = x_sample.reshape(db, d)
    st = {k: [] for k in ("cmp_p", "cmp_s", "slc_p", "slc_s", "win_p", "win_s", "dn_p", "dn_s", "dnc_p", "dnc_s",
                          "cfc_p", "cfc_s")}
    for l in range(depth):
        wr = _pack_w_in(w_in[l])
        pe_flat, w1big, w2big = _pack_cmp_weights(nsa_cmp_pe[l], nsa_cmp_w1[l], nsa_cmp_w2[l])
        par = _pack_dn_par(dn_a_log[l], dn_dt_bias[l])
        ng = dn_norm_g[l].reshape(1, DN_DV)
        fbias, fg, fb = cf_dw_b[l].reshape(1, CF_C), cf_ln_g[l].reshape(1, CF_C), cf_ln_b[l].reshape(1, CF_C)
        pa, pb, pc, wo = (w_branch_a[l].astype(BF), w_branch_b[l].astype(BF), w_branch_c[l].astype(BF),
                          w_out[l].astype(BF))
        wi_f, wo_f = _pack_ffn(w_ffn_in[l], w_ffn_out[l])
        g1, b1, g2, b2 = (ln1_g[l].reshape(1, d), ln1_b[l].reshape(1, d), ln2_g[l].reshape(1, d),
                          ln2_b[l].reshape(1, d))

        u_p = _matmul(xp, wr, tm_p, 1536)
        oa, dn_state = _dn_prompt(u_p, dn_conv_w[l], par, ng, bsz, t)
        ob, cf_tail = _cf_prompt(u_p, cf_dw_w[l], fbias, fg, fb, bsz, t)
        kv = u_p[:, P_NKV:P_NKV + NSA_KV]
        kv5 = kv.reshape(bsz, t, 3, 2, NSA_G, NSA_DH)
        kvc = _compress_prompt(kv[:, :NSA_ROW].reshape(bsz * t // L_CMP, CMP_FLAT), pe_flat, w1big, w2big,
                               min(128, bsz * t // L_CMP))
        oc = _nsa_prompt(u_p, kvc.reshape(bsz, t // L_CMP, NSA_ROW), kv.astype(BF).reshape(bsz, t, NSA_KV), bsz, t)
        x1 = _merge(xp, oa, ob, oc, u_p, pa, pb, pc, wo, g1, b1, 256, alpha)
        xp_next = _ffn(x1, wi_f, wo_f, g2, b2, tm_p, alpha)
        st["cmp_p"].append(kv5[:, :, 0])
        st["slc_p"].append(kv5[:, :, 1])
        st["win_p"].append(kv5[:, -min(WINDOW, t):, 2])
        st["dn_p"].append(dn_state)
        st["dnc_p"].append(u_p[:, P_DNQKV:P_DNQKV + DN_CONV_C].reshape(bsz, t, DN_CONV_C)[:, -(DN_CONV - 1):])
        st["cfc_p"].append(cf_tail[:, CF_HALO - (CF_K - 1):])

        u_s = _matmul(xs, wr, db, 1536)
        u_s3 = u_s.reshape(db, 1, N_PAD)
        oa_s, ob_s, dn_state_s, glu_s = _sample_mix(u_s3, state_dn, state_dn_conv, state_cf_conv, l, dn_conv_w[l],
                                                    par, ng, cf_dw_w[l], fbias, fg, fb, db)
        kvc_s = _compress_sample(cmp4, pt_flat, l, pe_flat, w1big, w2big, db, n_pages)
        n_sel = n_pages * (PAGE_SIZE // L_SEL) + 1
        ocmp_s, idx = _sample_select(u_s3, kvc_s, db, past, n_sel)
        oc_s = _sample_attn(idx[:, :, :N_TOP].reshape(-1), pt_flat, slc_half, u_s3, win4, ocmp_s, l, db, n_pages,
                            past, l_buf)
        x1s = _merge(xs, oa_s.reshape(db, DN_W), ob_s.reshape(db, CF_C), oc_s.reshape(db, NSA_W), u_s, pa, pb, pc,
                     wo, g1, b1, db, alpha)
        xs_next = _ffn(x1s, wi_f, wo_f, g2, b2, db, alpha)
        kv_s = u_s[:, P_NKV:P_NKV + NSA_KV].reshape(db, 1, 3, 2, NSA_G, NSA_DH)
        st["cmp_s"].append(kv_s[:, :, 0])
        st["slc_s"].append(kv_s[:, :, 1])
        st["win_s"].append(jnp.concatenate([cache_nsa_win[l], kv_s[:, :, 2]], axis=1)[:, -min(WINDOW, l_buf + 1):])
        st["dn_s"].append(dn_state_s)
        st["dnc_s"].append(jnp.concatenate([state_dn_conv[l], u_s[:, None, P_DNQKV:P_DNQKV + DN_CONV_C]],
                                           axis=1)[:, -(DN_CONV - 1):])
        st["cfc_s"].append(jnp.concatenate([state_cf_conv[l], glu_s], axis=1)[:, -(CF_K - 1):])
        xp, xs = xp_next, xs_next

    stk = {k: jnp.stack(v) for k, v in st.items()}
    return (xp.reshape(bsz, t, d), xs.reshape(db, 1, d), stk["cmp_p"], stk["cmp_s"], stk["slc_p"], stk["slc_s"],
            stk["win_p"], stk["win_s"], stk["dn_p"], stk["dn_s"], stk["dnc_p"], stk["dnc_s"], stk["cfc_p"],
            stk["cfc_s"])
```

```python
import functools

import jax
import jax.numpy as jnp
from jax import lax
from jax.experimental import pallas as pl
from jax.experimental.pallas import tpu as pltpu

F32 = jnp.float32
BF = jnp.bfloat16
HI = lax.Precision.HIGHEST

D_MODEL = 1024
DN_H = 4
DN_DK = 128
DN_DV = 128
DN_QK = DN_H * DN_DK
DN_W = DN_H * DN_DV
DN_CONV_C = 2 * DN_QK + DN_W
DN_CONV = 4
DN_CHUNK = 64
CF_C = 512
CF_K = 31
NSA_H = 8
NSA_G = 2
NSA_HPG = NSA_H // NSA_G
NSA_DH = 64
NSA_W = NSA_H * NSA_DH
NSA_KV = 3 * 2 * NSA_G * NSA_DH
NSA_ROW = 2 * NSA_G * NSA_DH
L_CMP = 32
L_SEL = 64
N_TOP = 16
WINDOW = 512
Q_BLOCK = 128
FORCE_SCORE = 1.0e4
PAGE_SIZE = 128
D_FF = 2816
FFN_CHUNKS = 2
NEG_BIG = -1.0e30

OFF_DN_Z = DN_CONV_C
OFF_DN_B = OFF_DN_Z + DN_W
OFF_DN_A = OFF_DN_B + DN_H
OFF_CF = OFF_DN_A + DN_H
OFF_NSA_Q = OFF_CF + 2 * CF_C
OFF_NSA_KV = OFF_NSA_Q + NSA_W
OFF_NSA_G = OFF_NSA_KV + NSA_KV
OFF_MERGE = OFF_NSA_G + 3 * NSA_H
N_IN = OFF_MERGE + 3 * D_MODEL

P_MERGE = 0
P_DNQKV = 3072
P_NKV = 4608
P_SMALL = 5376
P_DNZ = 5632
P_CF = 6144
P_NQ = 7168
N_PAD = 7680
SM_B = 0
SM_A = DN_H
SM_G = 2 * DN_H

def _alpha(depth):
    return (2 * depth) ** 0.25


def _dot(a, b, precision=None):
    return jnp.dot(a, b, preferred_element_type=F32, precision=precision)


def _dot_nt(a, b):
    return lax.dot_general(a, b, (((1,), (1,)), ((), ())), preferred_element_type=F32)


def _dot_tn(a, b):
    return lax.dot_general(a, b, (((0,), (0,)), ((), ())), preferred_element_type=F32)


def _silu(x):
    return x * jax.nn.sigmoid(x)


def _softplus(x):
    return jnp.maximum(x, 0.0) + jnp.log(1.0 + jnp.exp(-jnp.abs(x)))


def _l2norm(x):
    return x * lax.rsqrt(jnp.sum(x * x, -1, keepdims=True) + 1e-6)


def _layer_norm(x, g, b):
    xc = x - jnp.mean(x, -1, keepdims=True)
    var = jnp.mean(xc * xc, -1, keepdims=True)
    return xc * lax.rsqrt(var + 1e-5) * g + b


def _masked_softmax(s, mask):
    s = jnp.where(mask, s, -jnp.inf)
    m = jnp.max(s, -1, keepdims=True)
    m = jnp.where(m > -jnp.inf, m, 0.0)
    e = jnp.exp(s - m)
    den = jnp.sum(e, -1, keepdims=True)
    return e / jnp.where(den > 0, den, 1.0)


def _params(*sem, vmem_mb=None):
    kw = {}
    if vmem_mb is not None:
        kw["vmem_limit_bytes"] = vmem_mb * 1024 * 1024
    return pltpu.CompilerParams(dimension_semantics=sem, **kw)


def _mm_kernel(x_ref, w_ref, o_ref):
    o_ref[...] = _dot(x_ref[...].astype(BF), w_ref[...])


def _matmul(x, w, tm, tn):
    m, k = x.shape
    n = w.shape[1]
    return pl.pallas_call(
        _mm_kernel,
        grid=(m // tm, n // tn),
        in_specs=[pl.BlockSpec((tm, k), lambda i, j: (i, 0)),
                  pl.BlockSpec((k, tn), lambda i, j: (0, j))],
        out_specs=pl.BlockSpec((tm, tn), lambda i, j: (i, j)),
        out_shape=jax.ShapeDtypeStruct((m, n), F32),
        compiler_params=_params("arbitrary", "arbitrary"),
        name="proj_matmul",
    )(x, w)


def _dn_gates(sm, par):
    beta = jax.nn.sigmoid(sm)
    g = -jnp.exp(par[0:1, :]) * _softplus(sm + par[1:2, :])
    return beta, g


def _dn_prompt_kernel(qkv_ref, z_ref, sm_ref, cw_ref, par_ref, ng_ref, oa_ref, st_ref, xp_ref, s_ref, *, n_chunks):
    n = pl.program_id(0)

    @pl.when(n == 0)
    def _():
        xp_ref[:, 0:8, :] = jnp.zeros((xp_ref.shape[0], 8, DN_CONV_C), F32)
        s_ref[...] = jnp.zeros_like(s_ref)

    for bi in range(qkv_ref.shape[0]):
        _dn_chunk(qkv_ref.at[bi], z_ref.at[bi], sm_ref.at[bi], cw_ref, par_ref, ng_ref, oa_ref.at[bi],
                  xp_ref.at[bi], s_ref.at[bi])

    @pl.when(n == n_chunks - 1)
    def _():
        st_ref[...] = s_ref[...]


def _dn_chunk(qkv_ref, z_ref, sm_ref, cw_ref, par_ref, ng_ref, oa_ref, xp_ref, s_ref):
    c = DN_CHUNK
    raw = qkv_ref[...]
    xp_ref[8:8 + c, :] = raw
    cw = cw_ref[...]
    conv = cw[0:1, :] * xp_ref[5:5 + c, :]
    for k in range(1, DN_CONV):
        conv = conv + cw[k:k + 1, :] * xp_ref[5 + k:5 + k + c, :]
    xp_ref[0:8, :] = raw[c - 8:c, :]
    h = _silu(conv)

    beta_all, g_all = _dn_gates(sm_ref[...], par_ref[...])
    ii = lax.broadcasted_iota(jnp.int32, (c, c), 0)
    jj = lax.broadcasted_iota(jnp.int32, (c, c), 1)
    tri = jnp.where(ii >= jj, 1.0, 0.0).astype(F32)
    eye = jnp.where(ii == jj, 1.0, 0.0).astype(F32)
    gcum = _dot(tri, g_all, precision=HI)
    gcum_t = gcum.T
    z = z_ref[...]
    ng = ng_ref[...]

    outs = []
    for hd in range(DN_H):
        qh = _l2norm(h[:, hd * DN_DK:(hd + 1) * DN_DK]) * (DN_DK ** -0.5)
        kh = _l2norm(h[:, DN_QK + hd * DN_DK:DN_QK + (hd + 1) * DN_DK])
        vh = h[:, 2 * DN_QK + hd * DN_DV:2 * DN_QK + (hd + 1) * DN_DV]
        bcol = beta_all[:, SM_B + hd:SM_B + hd + 1]
        gc = gcum[:, SM_A + hd:SM_A + hd + 1]
        gr = gcum_t[SM_A + hd:SM_A + hd + 1, :]
        diff = gc - gr
        dec_incl = jnp.exp(jnp.where(ii >= jj, diff, -jnp.inf))
        dec_strict = jnp.where(ii > jj, dec_incl, 0.0)
        kb = kh.astype(BF)
        kk = _dot_nt(kb, kb)
        nm = -(bcol * kk * dec_strict)
        inv = eye + nm
        pw = nm
        for _ in range(5):
            pw = _dot(pw, pw, precision=HI)
            inv = inv + _dot(inv, pw, precision=HI)
        eg = jnp.exp(gc)
        rhs = jnp.concatenate([bcol * vh, (bcol * eg) * kh], axis=1)
        xsol = _dot(inv, rhs, precision=HI)
        w_val = xsol[:, :DN_DV]
        k_cum = xsol[:, DN_DV:]
        qk = _dot_nt(qh.astype(BF), kb) * dec_incl
        q_dec = qh * eg
        g_last = gc[c - 1:c, :]
        k_dec = kh * jnp.exp(g_last - gc)
        s_old = s_ref[hd]
        sb = s_old.astype(BF)
        u = w_val - _dot(k_cum.astype(BF), sb)
        ub = u.astype(BF)
        o = _dot(q_dec.astype(BF), sb) + _dot(qk.astype(BF), ub)
        s_ref[hd] = jnp.exp(g_last) * s_old + _dot_tn(k_dec.astype(BF), ub)
        o = o * lax.rsqrt(jnp.mean(o * o, -1, keepdims=True) + 1e-6) * ng
        outs.append(o * _silu(z[:, hd * DN_DV:(hd + 1) * DN_DV]))
    oa_ref[...] = jnp.concatenate(outs, axis=1).astype(BF)


def _dn_prompt(u_p3, conv_w, par, ng):
    b, t, _ = u_p3.shape
    nc = t // DN_CHUNK
    c = DN_CHUNK
    return pl.pallas_call(
        functools.partial(_dn_prompt_kernel, n_chunks=nc),
        grid=(nc,),
        in_specs=[pl.BlockSpec((b, c, DN_CONV_C), lambda n: (0, n, P_DNQKV // DN_CONV_C)),
                  pl.BlockSpec((b, c, DN_W), lambda n: (0, n, P_DNZ // DN_W)),
                  pl.BlockSpec((b, c, 128), lambda n: (0, n, P_SMALL // 128)),
                  pl.BlockSpec((DN_CONV, DN_CONV_C), lambda n: (0, 0)),
                  pl.BlockSpec((8, 128), lambda n: (0, 0)),
                  pl.BlockSpec((1, DN_DV), lambda n: (0, 0))],
        out_specs=[pl.BlockSpec((b, c, DN_W), lambda n: (0, n, 0)),
                   pl.BlockSpec((b, DN_H, DN_DK, DN_DV), lambda n: (0, 0, 0, 0))],
        out_shape=[jax.ShapeDtypeStruct((b, t, DN_W), BF),
                   jax.ShapeDtypeStruct((b, DN_H, DN_DK, DN_DV), F32)],
        scratch_shapes=[pltpu.VMEM((b, 8 + c, DN_CONV_C), F32),
                        pltpu.VMEM((b, DN_H, DN_DK, DN_DV), F32)],
        compiler_params=_params("arbitrary"),
        name="dn_prompt",
    )(u_p3, u_p3, u_p3, conv_w, par, ng)


CF_TILE = 256
CF_ROWS = 32
CF_HALO = 32


def _cf_prompt_kernel(u_ref, w_ref, bias_ref, g_ref, b_ref, ob_ref, tail_ref, buf_ref, *, n_tiles):
    i = pl.program_id(1)
    tc = CF_TILE

    @pl.when(i == 0)
    def _():
        buf_ref[0:CF_HALO, :] = jnp.zeros((CF_HALO, CF_C), F32)

    u = u_ref[...]
    buf_ref[CF_HALO:CF_HALO + tc, :] = u[:, :CF_C] * jax.nn.sigmoid(u[:, CF_C:])
    bias = bias_ref[...]
    g = g_ref[...]
    b = b_ref[...]
    off = CF_HALO - (CF_K - 1)
    for r0 in range(0, tc, CF_ROWS):
        acc = bias + w_ref[0:1, :] * buf_ref[r0 + off:r0 + off + CF_ROWS, :]
        for k in range(1, CF_K):
            acc = acc + w_ref[k:k + 1, :] * buf_ref[r0 + off + k:r0 + off + k + CF_ROWS, :]
        ob_ref[r0:r0 + CF_ROWS, :] = _silu(_layer_norm(acc, g, b)).astype(BF)
    tail = buf_ref[tc:tc + CF_HALO, :]
    buf_ref[0:CF_HALO, :] = tail

    @pl.when(i == n_tiles - 1)
    def _():
        tail_ref[0] = tail


def _cf_prompt(u_p, w, bias, g, bb, b, t):
    nt = t // CF_TILE
    return pl.pallas_call(
        functools.partial(_cf_prompt_kernel, n_tiles=nt),
        grid=(b, nt),
        in_specs=[pl.BlockSpec((CF_TILE, 2 * CF_C), lambda i, n: (i * nt + n, P_CF // (2 * CF_C))),
                  pl.BlockSpec((CF_K, CF_C), lambda i, n: (0, 0)),
                  pl.BlockSpec((1, CF_C), lambda i, n: (0, 0)),
                  pl.BlockSpec((1, CF_C), lambda i, n: (0, 0)),
                  pl.BlockSpec((1, CF_C), lambda i, n: (0, 0))],
        out_specs=[pl.BlockSpec((CF_TILE, CF_C), lambda i, n: (i * nt + n, 0)),
                   pl.BlockSpec((1, CF_HALO, CF_C), lambda i, n: (i, 0, 0))],
        out_shape=[jax.ShapeDtypeStruct((b * t, CF_C), BF),
                   jax.ShapeDtypeStruct((b, CF_HALO, CF_C), F32)],
        scratch_shapes=[pltpu.VMEM((CF_HALO + CF_TILE, CF_C), F32)],
        compiler_params=_params("arbitrary", "arbitrary"),
        name="cf_prompt",
    )(u_p, w, bias, g, bb)


CMP_FLAT = L_CMP * NSA_ROW


def _cmp_mlp(x, pe, w1, w2):
    xb = (x + pe).astype(BF)
    h = _silu(_dot(xb, w1))
    return _dot(h.astype(BF), w2)


def _cmp_kernel(x_ref, pe_ref, w1_ref, w2_ref, o_ref, ot_ref):
    kvc = _cmp_mlp(x_ref[0], pe_ref[...], w1_ref[...], w2_ref[...])
    o_ref[0] = kvc
    ot_ref[0] = kvc.T


def _compress_prompt(x, pe, w1, w2):
    b, nc, _ = x.shape
    return pl.pallas_call(
        _cmp_kernel,
        grid=(b,),
        in_specs=[pl.BlockSpec((1, nc, CMP_FLAT), lambda i: (i, 0, 0)),
                  pl.BlockSpec((1, CMP_FLAT), lambda i: (0, 0)),
                  pl.BlockSpec((CMP_FLAT, NSA_ROW), lambda i: (0, 0)),
                  pl.BlockSpec((NSA_ROW, NSA_ROW), lambda i: (0, 0))],
        out_specs=[pl.BlockSpec((1, nc, NSA_ROW), lambda i: (i, 0, 0)),
                   pl.BlockSpec((1, NSA_ROW, nc), lambda i: (i, 0, 0))],
        out_shape=[jax.ShapeDtypeStruct((b, nc, NSA_ROW), F32),
                   jax.ShapeDtypeStruct((b, NSA_ROW, nc), F32)],
        compiler_params=_params("arbitrary"),
        name="nsa_compress_prompt",
    )(x, pe, w1, w2)


NT_ROWS = NSA_W + NSA_KV


def _proj_nt_kernel(x_ref, w_ref, obf_ref, okv_ref):
    r = _dot_nt(w_ref[...], x_ref[0].astype(BF))
    obf_ref[0] = r.astype(BF)
    okv_ref[0] = r[NSA_W:, :]


def _proj_nt(x3, w_t, tm):
    b, t, d = x3.shape
    return pl.pallas_call(
        _proj_nt_kernel,
        grid=(b, t // tm),
        in_specs=[pl.BlockSpec((1, tm, d), lambda i, j: (i, j, 0)),
                  pl.BlockSpec((NT_ROWS, d), lambda i, j: (0, 0))],
        out_specs=[pl.BlockSpec((1, NT_ROWS, tm), lambda i, j: (i, 0, j)),
                   pl.BlockSpec((1, NSA_KV, tm), lambda i, j: (i, 0, j))],
        out_shape=[jax.ShapeDtypeStruct((b, NT_ROWS, t), BF),
                   jax.ShapeDtypeStruct((b, NSA_KV, t), F32)],
        compiler_params=_params("arbitrary", "arbitrary"),
        name="proj_qkv_t",
    )(x3, w_t)


CMP_PAGES = 16
BLK_PER_PAGE = PAGE_SIZE // L_CMP


def _cmp_sample_kernel(pt_ref, *refs):
    pages = refs[:CMP_PAGES]
    pe_ref, w1_ref, w2_ref, o_ref, xs_ref = refs[CMP_PAGES:]
    for i in range(CMP_PAGES):
        xs_ref[BLK_PER_PAGE * i:BLK_PER_PAGE * (i + 1), :] = pages[i][0, 0]
    o_ref[0] = _cmp_mlp(xs_ref[...], pe_ref[...], w1_ref[...], w2_ref[...])


def _compress_sample(cache4, pt_flat, layer, pe, w1, w2, db, n_pages):
    steps = n_pages // CMP_PAGES
    rows = CMP_PAGES * BLK_PER_PAGE

    def page_map(i):
        return lambda b, j, pt: (layer, pt[b * n_pages + j * CMP_PAGES + i], 0, 0)

    grid_spec = pltpu.PrefetchScalarGridSpec(
        num_scalar_prefetch=1,
        grid=(db, steps),
        in_specs=[pl.BlockSpec((1, 1, BLK_PER_PAGE, CMP_FLAT), page_map(i)) for i in range(CMP_PAGES)]
        + [pl.BlockSpec((1, CMP_FLAT), lambda b, j, pt: (0, 0)),
           pl.BlockSpec((CMP_FLAT, NSA_ROW), lambda b, j, pt: (0, 0)),
           pl.BlockSpec((NSA_ROW, NSA_ROW), lambda b, j, pt: (0, 0))],
        out_specs=pl.BlockSpec((1, rows, NSA_ROW), lambda b, j, pt: (b, j, 0)),
        scratch_shapes=[pltpu.VMEM((rows, CMP_FLAT), F32)],
    )
    return pl.pallas_call(
        _cmp_sample_kernel,
        grid_spec=grid_spec,
        out_shape=jax.ShapeDtypeStruct((db, n_pages * BLK_PER_PAGE, NSA_ROW), F32),
        compiler_params=_params("arbitrary", "arbitrary"),
        name="nsa_compress_sample",
    )(pt_flat, *([cache4] * CMP_PAGES), pe, w1, w2)


SEL_TK = 512


def _pair_matrix(nc, n_sel_pad):
    ci = lax.broadcasted_iota(jnp.int32, (nc, n_sel_pad), 0)
    si = lax.broadcasted_iota(jnp.int32, (nc, n_sel_pad), 1)
    return jnp.where((ci >> 1) == si, 1.0, 0.0).astype(F32)


def _head_gate(gates, g, hp, br):
    col = SM_G + (g * NSA_HPG + hp) * 3 + br
    return gates[:, col:col + 1]


def _masked_softmax0(s, mask):
    s = jnp.where(mask, s, -jnp.inf)
    m = jnp.max(s, 0, keepdims=True)
    m = jnp.where(m > -jnp.inf, m, 0.0)
    e = jnp.exp(s - m)
    den = jnp.sum(e, 0, keepdims=True)
    return e / jnp.where(den > 0, den, 1.0)


def _lane_tile(x, n):
    return jnp.concatenate([x] * n, axis=1)


def _nsa_prompt_kernel(qt_ref, sm_ref, kvc_ref, kvct_ref, ks_ref, kw_ref, vst_ref, vwt_ref, o_ref, sel_ref, *, t_len):
    i = pl.program_id(1)
    qb = Q_BLOCK
    cols = NSA_HPG * qb
    start = i * qb
    nc = t_len // L_CMP
    n_sel = t_len // L_SEL
    blk_per_tile = SEL_TK // L_SEL
    qt = qt_ref[0] * (NSA_DH ** -0.5)
    gates_t = jax.nn.sigmoid(sm_ref[...]).T
    t1 = start + lax.broadcasted_iota(jnp.int32, (1, qb), 1)
    t4 = start + (lax.broadcasted_iota(jnp.int32, (1, cols), 1) & (qb - 1))
    q4 = [jnp.concatenate([qt[(g * NSA_HPG + hp) * NSA_DH:(g * NSA_HPG + hp + 1) * NSA_DH, :]
                           for hp in range(NSA_HPG)], axis=1) for g in range(NSA_G)]
    zq = jnp.zeros((NSA_DH, cols), BF)
    qbd = jnp.concatenate([jnp.concatenate([q4[0], zq], axis=1), jnp.concatenate([zq, q4[1]], axis=1)], axis=0)

    kvc = kvc_ref[0]
    kvct = kvct_ref[0]
    s_c = _dot(kvc[:, 0:NSA_G * NSA_DH].astype(BF), qbd)
    c_end = (lax.broadcasted_iota(jnp.int32, (nc, 1), 0) + 1) * L_CMP - 1
    m_c = c_end <= t4
    pair_t = jnp.where((lax.broadcasted_iota(jnp.int32, (n_sel, nc), 1) >> 1)
                       == lax.broadcasted_iota(jnp.int32, (n_sel, nc), 0), 1.0, 0.0).astype(F32)
    blk2 = lax.broadcasted_iota(jnp.int32, (n_sel, qb), 0)
    cur = t1 >> 6
    valid = blk2 * L_SEL <= t1
    forced = (blk2 == 0) | (blk2 == cur) | (blk2 == cur - 1)
    o_cmp = []
    for g in range(NSA_G):
        p_c = _masked_softmax0(s_c[:, g * cols:(g + 1) * cols], m_c)
        vct = kvct[NSA_G * NSA_DH + g * NSA_DH:NSA_G * NSA_DH + (g + 1) * NSA_DH, :].astype(BF)
        o_cmp.append(_dot(vct, p_c.astype(BF)))
        imp_c = p_c[:, 0:qb] + p_c[:, qb:2 * qb] + p_c[:, 2 * qb:3 * qb] + p_c[:, 3 * qb:4 * qb]
        imp = _dot(pair_t, imp_c, precision=HI)
        score = jnp.where(valid, jnp.where(forced, FORCE_SCORE, imp), -jnp.inf)
        rank = jnp.zeros((n_sel, qb), F32)
        for r in range(n_sel):
            sr = score[r:r + 1, :]
            tie = jnp.where(blk2 > r, 1.0, 0.0)
            rank = rank + jnp.where(sr > score, 1.0, jnp.where(sr == score, tie, 0.0))
        sel_ref[g] = jnp.where(rank < N_TOP, jnp.where(score > -jnp.inf, 1.0, 0.0), 0.0)

    n_kt = (start + qb + SEL_TK - 1) // SEL_TK

    def body(j, carry):
        k0 = pl.multiple_of(j * SEL_TK, SEL_TK)
        b0 = pl.multiple_of(j * blk_per_tile, blk_per_tile)
        s_both = _dot(ks_ref[0, pl.ds(k0, SEL_TK), :], qbd)
        causal = (k0 + lax.broadcasted_iota(jnp.int32, (SEL_TK, 1), 0)) <= t1
        out = []
        for g in range(NSA_G):
            m_i, l_i, acc = carry[3 * g:3 * g + 3]
            st = sel_ref[g, pl.ds(b0, blk_per_tile), :]
            mexp = jnp.concatenate([jnp.broadcast_to(st[bb:bb + 1, :], (L_SEL, qb)) for bb in range(blk_per_tile)],
                                   axis=0)
            mask = _lane_tile(jnp.where(causal, mexp, 0.0), NSA_HPG) > 0.5
            s = jnp.where(mask, s_both[:, g * cols:(g + 1) * cols], NEG_BIG)
            m_new = jnp.maximum(m_i, jnp.max(s, 0, keepdims=True))
            alpha = jnp.exp(m_i - m_new)
            p = jnp.exp(s - m_new)
            l_new = alpha * l_i + jnp.sum(p, 0, keepdims=True)
            vt = vst_ref[0, g * NSA_DH:(g + 1) * NSA_DH, pl.ds(k0, SEL_TK)]
            out += [m_new, l_new, alpha * acc + _dot(vt, p.astype(BF))]
        return tuple(out)

    init = (jnp.full((1, cols), NEG_BIG, F32), jnp.zeros((1, cols), F32), jnp.zeros((NSA_DH, cols), F32)) * NSA_G
    res = lax.fori_loop(0, n_kt, body, init)
    o_slc = [res[3 * g + 2] / jnp.where(res[3 * g + 1] > 0, res[3 * g + 1], 1.0) for g in range(NSA_G)]

    kstart = pl.multiple_of(jnp.maximum(start - WINDOW, 0), qb)
    wlen = WINDOW + qb
    s_w = _dot(kw_ref[0, pl.ds(kstart, wlen), :], qbd)
    dist = t4 - (kstart + lax.broadcasted_iota(jnp.int32, (wlen, 1), 0))
    m_w = (dist >= 0) & (dist <= WINDOW)
    outs = []
    for g in range(NSA_G):
        p_w = _masked_softmax0(s_w[:, g * cols:(g + 1) * cols], m_w)
        o_win = _dot(vwt_ref[0, g * NSA_DH:(g + 1) * NSA_DH, pl.ds(kstart, wlen)], p_w.astype(BF))
        for hp in range(NSA_HPG):
            c = slice(hp * qb, (hp + 1) * qb)
            row = SM_G + (g * NSA_HPG + hp) * 3
            outs.append(gates_t[row:row + 1, :] * o_cmp[g][:, c] + gates_t[row + 1:row + 2, :] * o_slc[g][:, c]
                        + gates_t[row + 2:row + 3, :] * o_win[:, c])
    o_ref[...] = jnp.concatenate(outs, axis=0).T.astype(BF)


def _nsa_prompt(u_p, qkvt_bf, kvc, kvct, kv_bf, b, t):
    nq = t // Q_BLOCK
    nc = t // L_CMP
    k_blk = NSA_G * NSA_DH
    slc_k = NSA_ROW // k_blk
    win_k = 2 * NSA_ROW // k_blk
    slc_v = (NSA_W + NSA_ROW + k_blk) // k_blk
    win_v = (NSA_W + 2 * NSA_ROW + k_blk) // k_blk
    return pl.pallas_call(
        functools.partial(_nsa_prompt_kernel, t_len=t),
        grid=(b, nq),
        in_specs=[pl.BlockSpec((1, NSA_W, Q_BLOCK), lambda i, n: (i, 0, n)),
                  pl.BlockSpec((Q_BLOCK, 128), lambda i, n: (i * nq + n, P_SMALL // 128)),
                  pl.BlockSpec((1, nc, NSA_ROW), lambda i, n: (i, 0, 0)),
                  pl.BlockSpec((1, NSA_ROW, nc), lambda i, n: (i, 0, 0)),
                  pl.BlockSpec((1, t, k_blk), lambda i, n: (i, 0, slc_k)),
                  pl.BlockSpec((1, t, k_blk), lambda i, n: (i, 0, win_k)),
                  pl.BlockSpec((1, k_blk, t), lambda i, n: (i, slc_v, 0)),
                  pl.BlockSpec((1, k_blk, t), lambda i, n: (i, win_v, 0))],
        out_specs=pl.BlockSpec((Q_BLOCK, NSA_W), lambda i, n: (i * nq + n, 0)),
        out_shape=jax.ShapeDtypeStruct((b * t, NSA_W), BF),
        scratch_shapes=[pltpu.VMEM((NSA_G, t // L_SEL, Q_BLOCK), F32)],
        compiler_params=_params("arbitrary", "arbitrary", vmem_mb=48),
        name="nsa_prompt",
    )(qkvt_bf, u_p, kvc, kvct, kv_bf, kv_bf, qkvt_bf, qkvt_bf)


def _merge_kernel(x_ref, oa_ref, ob_ref, oc_ref, gm_ref, pa_ref, pb_ref, pc_ref, wo_ref, g_ref, b_ref, o_ref, *, alpha):
    gm = jax.nn.sigmoid(gm_ref[...])
    d = D_MODEL
    mix_in = (gm[:, 0:d] * _dot(oa_ref[...], pa_ref[...]) + gm[:, d:2 * d] * _dot(ob_ref[...], pb_ref[...])
              + gm[:, 2 * d:3 * d] * _dot(oc_ref[...], pc_ref[...]))
    mix = _dot(mix_in.astype(BF), wo_ref[...])
    o_ref[...] = _layer_norm(alpha * x_ref[...] + mix, g_ref[...], b_ref[...])


def _merge(x, oa, ob, oc, u, pa, pb, pc, wo, g, b, tm, alpha):
    m = x.shape[0]
    row = lambda i: (i, 0)
    fix = lambda i: (0, 0)
    return pl.pallas_call(
        functools.partial(_merge_kernel, alpha=alpha),
        grid=(m // tm,),
        in_specs=[pl.BlockSpec((tm, D_MODEL), row),
                  pl.BlockSpec((tm, DN_W), row), pl.BlockSpec((tm, CF_C), row), pl.BlockSpec((tm, NSA_W), row),
                  pl.BlockSpec((tm, 3 * D_MODEL), lambda i: (i, P_MERGE // (3 * D_MODEL))),
                  pl.BlockSpec((DN_W, D_MODEL), fix), pl.BlockSpec((CF_C, D_MODEL), fix),
                  pl.BlockSpec((NSA_W, D_MODEL), fix), pl.BlockSpec((D_MODEL, D_MODEL), fix),
                  pl.BlockSpec((1, D_MODEL), fix), pl.BlockSpec((1, D_MODEL), fix)],
        out_specs=pl.BlockSpec((tm, D_MODEL), row),
        out_shape=jax.ShapeDtypeStruct((m, D_MODEL), F32),
        compiler_params=_params("arbitrary", vmem_mb=48),
        name="merge_ln",
    )(x, oa, ob, oc, u, pa, pb, pc, wo, g, b)


def _ffn_kernel(x_ref, wi_ref, wo_ref, g_ref, b_ref, o_ref, acc_ref, *, alpha):
    j = pl.program_id(1)
    fc = D_FF // FFN_CHUNKS
    x = x_ref[...]
    h = _dot(x.astype(BF), wi_ref[0])
    y = _dot((_silu(h[:, :fc]) * h[:, fc:]).astype(BF), wo_ref[0])

    @pl.when(j == 0)
    def _():
        acc_ref[...] = y

    @pl.when(j > 0)
    def _():
        acc_ref[...] += y

    @pl.when(j == FFN_CHUNKS - 1)
    def _():
        o_ref[...] = _layer_norm(alpha * x + acc_ref[...], g_ref[...], b_ref[...])


def _ffn(x, wi, wo, g, b, tm, alpha):
    m = x.shape[0]
    fc = D_FF // FFN_CHUNKS
    return pl.pallas_call(
        functools.partial(_ffn_kernel, alpha=alpha),
        grid=(m // tm, FFN_CHUNKS),
        in_specs=[pl.BlockSpec((tm, D_MODEL), lambda i, j: (i, 0)),
                  pl.BlockSpec((1, D_MODEL, 2 * fc), lambda i, j: (j, 0, 0)),
                  pl.BlockSpec((1, fc, D_MODEL), lambda i, j: (j, 0, 0)),
                  pl.BlockSpec((1, D_MODEL), lambda i, j: (0, 0)),
                  pl.BlockSpec((1, D_MODEL), lambda i, j: (0, 0))],
        out_specs=pl.BlockSpec((tm, D_MODEL), lambda i, j: (i, 0)),
        out_shape=jax.ShapeDtypeStruct((m, D_MODEL), F32),
        scratch_shapes=[pltpu.VMEM((tm, D_MODEL), F32)],
        compiler_params=_params("arbitrary", "arbitrary", vmem_mb=48),
        name="ffn_ln",
    )(x, wi, wo, g, b)


def _sample_mix_kernel(qkv_ref, z_ref, sm_ref, cf_ref, st_ref, dcs_ref, cfs_ref, cw_ref, par_ref, ng_ref,
                       fw_ref, fbias_ref, fg_ref, fb_ref, oa_ref, ob_ref, so_ref, glu_ref):
    cs = dcs_ref[0, 0]
    new = qkv_ref[0]
    cw = cw_ref[...]
    conv = cw[DN_CONV - 1:DN_CONV, :] * new
    for k in range(DN_CONV - 1):
        conv = conv + cw[k:k + 1, :] * cs[k:k + 1, :]
    h = _silu(conv)
    beta_all, g_all = _dn_gates(sm_ref[0], par_ref[...])
    z = z_ref[0]
    ng = ng_ref[...]
    outs = []
    for hd in range(DN_H):
        qh = _l2norm(h[:, hd * DN_DK:(hd + 1) * DN_DK]) * (DN_DK ** -0.5)
        kh = _l2norm(h[:, DN_QK + hd * DN_DK:DN_QK + (hd + 1) * DN_DK])
        vh = h[:, 2 * DN_QK + hd * DN_DV:2 * DN_QK + (hd + 1) * DN_DV]
        beta = beta_all[:, SM_B + hd:SM_B + hd + 1]
        g = g_all[:, SM_A + hd:SM_A + hd + 1]
        kcol = jnp.broadcast_to(kh, (DN_DK, DN_DK)).T
        qcol = jnp.broadcast_to(qh, (DN_DK, DN_DK)).T
        s = jnp.exp(g) * st_ref[0, 0, hd]
        u = beta * (vh - jnp.sum(s * kcol, 0, keepdims=True))
        s = s + kcol * u
        so_ref[0, hd] = s
        o = jnp.sum(s * qcol, 0, keepdims=True)
        o = o * lax.rsqrt(jnp.mean(o * o, -1, keepdims=True) + 1e-6) * ng
        outs.append(o * _silu(z[:, hd * DN_DV:(hd + 1) * DN_DV]))
    oa_ref[0] = jnp.concatenate(outs, axis=1).astype(BF)

    ucf = cf_ref[0]
    glu = ucf[:, :CF_C] * jax.nn.sigmoid(ucf[:, CF_C:])
    glu_ref[0] = glu
    fw = fw_ref[...]
    acc = fbias_ref[...] + jnp.sum(fw[0:CF_K - 1, :] * cfs_ref[0, 0], 0, keepdims=True) + fw[CF_K - 1:CF_K, :] * glu
    ob_ref[0] = _silu(_layer_norm(acc, fg_ref[...], fb_ref[...])).astype(BF)


def _sample_mix(u_s3, state_dn, state_dn_conv, state_cf_conv, layer, conv_w, par, ng, fw, fbias, fg, fb, db):
    fix = lambda b: (0, 0)
    return pl.pallas_call(
        _sample_mix_kernel,
        grid=(db,),
        in_specs=[pl.BlockSpec((1, 1, DN_CONV_C), lambda b: (b, 0, P_DNQKV // DN_CONV_C)),
                  pl.BlockSpec((1, 1, DN_W), lambda b: (b, 0, P_DNZ // DN_W)),
                  pl.BlockSpec((1, 1, 128), lambda b: (b, 0, P_SMALL // 128)),
                  pl.BlockSpec((1, 1, 2 * CF_C), lambda b: (b, 0, P_CF // (2 * CF_C))),
                  pl.BlockSpec((1, 1, DN_H, DN_DK, DN_DV), lambda b: (layer, b, 0, 0, 0)),
                  pl.BlockSpec((1, 1, DN_CONV - 1, DN_CONV_C), lambda b: (layer, b, 0, 0)),
                  pl.BlockSpec((1, 1, CF_K - 1, CF_C), lambda b: (layer, b, 0, 0)),
                  pl.BlockSpec((DN_CONV, DN_CONV_C), fix),
                  pl.BlockSpec((8, 128), fix),
                  pl.BlockSpec((1, DN_DV), fix),
                  pl.BlockSpec((CF_K, CF_C), fix),
                  pl.BlockSpec((1, CF_C), fix), pl.BlockSpec((1, CF_C), fix), pl.BlockSpec((1, CF_C), fix)],
        out_specs=[pl.BlockSpec((1, 1, DN_W), lambda b: (b, 0, 0)),
                   pl.BlockSpec((1, 1, CF_C), lambda b: (b, 0, 0)),
                   pl.BlockSpec((1, DN_H, DN_DK, DN_DV), lambda b: (b, 0, 0, 0)),
                   pl.BlockSpec((1, 1, CF_C), lambda b: (b, 0, 0))],
        out_shape=[jax.ShapeDtypeStruct((db, 1, DN_W), BF),
                   jax.ShapeDtypeStruct((db, 1, CF_C), BF),
                   jax.ShapeDtypeStruct((db, DN_H, DN_DK, DN_DV), F32),
                   jax.ShapeDtypeStruct((db, 1, CF_C), F32)],
        compiler_params=_params("arbitrary"),
        name="sample_dn_cf",
    )(u_s3, u_s3, u_s3, u_s3, state_dn, state_dn_conv, state_cf_conv, conv_w, par, ng, fw, fbias, fg, fb)


def _sample_q4(q, g):
    rows = [q[:, (g * NSA_HPG + hp) * NSA_DH:(g * NSA_HPG + hp + 1) * NSA_DH] for hp in range(NSA_HPG)]
    return jnp.concatenate(rows + [jnp.zeros((8 - NSA_HPG, NSA_DH), F32)], axis=0)


def _sample_sel_kernel(q_ref, kvc_ref, ocmp_ref, idx_ref, *, t_pos, n_sel, n_sel_pad):
    q = q_ref[0] * (NSA_DH ** -0.5)
    kvc = kvc_ref[0]
    nc = kvc.shape[0]
    c_end = (lax.broadcasted_iota(jnp.int32, (1, nc), 1) + 1) * L_CMP - 1
    pair = _pair_matrix(nc, n_sel_pad)
    blk = lax.broadcasted_iota(jnp.int32, (1, n_sel_pad), 1)
    cur = t_pos // L_SEL
    valid = (blk < n_sel) & (blk * L_SEL <= t_pos)
    forced = (blk == 0) | (blk == cur) | (blk == cur - 1)
    lane = lax.broadcasted_iota(jnp.int32, (1, 128), 1)
    o_parts = []
    idx_rows = []
    for g in range(NSA_G):
        q4 = _sample_q4(q, g).astype(BF)
        kcol = slice(g * NSA_DH, (g + 1) * NSA_DH)
        vcol = slice(NSA_G * NSA_DH + g * NSA_DH, NSA_G * NSA_DH + (g + 1) * NSA_DH)
        s_c = _dot_nt(q4, kvc[:, kcol].astype(BF))
        p_c = _masked_softmax(s_c, c_end <= t_pos)
        o_cmp = _dot(p_c.astype(BF), kvc[:, vcol].astype(BF))
        imp_c = p_c[0:1] + p_c[1:2] + p_c[2:3] + p_c[3:4]
        imp = _dot(jnp.broadcast_to(imp_c, (8, nc)), pair, precision=HI)[0:1]
        score = jnp.where(valid, jnp.where(forced, FORCE_SCORE, imp), -jnp.inf)
        idx_vec = jnp.full((1, 128), -1, jnp.int32)
        for r in range(min(N_TOP, n_sel)):
            m = jnp.max(score, -1, keepdims=True)
            first = jnp.min(jnp.where(score == m, blk, n_sel_pad), -1, keepdims=True)
            idx_vec = jnp.where((lane == r) & (m > -jnp.inf), first, idx_vec)
            score = jnp.where(blk == first, -jnp.inf, score)
        idx_rows.append(idx_vec)
        o_parts += [o_cmp[hp:hp + 1, :] for hp in range(NSA_HPG)]
    ocmp_ref[0] = jnp.concatenate(o_parts, axis=1)
    idx_ref[0] = jnp.concatenate(idx_rows, axis=0)


def _sample_select(u_s3, kvc_s, db, t_pos, n_sel):
    nc = kvc_s.shape[1]
    n_sel_pad = -(-n_sel // 128) * 128
    return pl.pallas_call(
        functools.partial(_sample_sel_kernel, t_pos=t_pos, n_sel=n_sel, n_sel_pad=n_sel_pad),
        grid=(db,),
        in_specs=[pl.BlockSpec((1, 1, NSA_W), lambda b: (b, 0, P_NQ // NSA_W)),
                  pl.BlockSpec((1, nc, NSA_ROW), lambda b: (b, 0, 0))],
        out_specs=[pl.BlockSpec((1, 1, NSA_W), lambda b: (b, 0, 0)),
                   pl.BlockSpec((1, NSA_G, 128), lambda b: (b, 0, 0))],
        out_shape=[jax.ShapeDtypeStruct((db, 1, NSA_W), F32),
                   jax.ShapeDtypeStruct((db, NSA_G, 128), jnp.int32)],
        compiler_params=_params("arbitrary"),
        name="nsa_sample_select",
    )(u_s3, kvc_s)


def _sample_attn_kernel(idx_ref, pt_ref, *refs, t_pos, n_past_blk, l_buf):
    nblk = NSA_G * N_TOP
    blks = refs[:nblk]
    q_ref, sm_ref, nkv_ref, win_ref, ocmp_ref, o_ref = refs[nblk:]
    b = pl.program_id(0)
    q = q_ref[0] * (NSA_DH ** -0.5)
    gates = jax.nn.sigmoid(sm_ref[0])
    newkv = nkv_ref[0]
    ocmp = ocmp_ref[0]
    sub_pp = PAGE_SIZE // L_SEL
    nkeys = N_TOP * PAGE_SIZE
    lane = lax.broadcasted_iota(jnp.int32, (1, nkeys), 1)
    kslot = lane // PAGE_SIZE
    prow = lane % PAGE_SIZE
    new_visible = n_past_blk * L_SEL <= t_pos
    wpos = t_pos - l_buf + lax.broadcasted_iota(jnp.int32, (1, l_buf), 1)
    wdist = t_pos - wpos
    m_w = (wdist >= 0) & (wdist <= WINDOW) & (wpos >= 0)

    def with_new_key(s, s_n, vt, v_n):
        m = jnp.maximum(jnp.max(s, -1, keepdims=True), s_n)
        m = jnp.where(m > -jnp.inf, m, 0.0)
        e = jnp.exp(s - m)
        e_n = jnp.exp(s_n - m)
        den = jnp.sum(e, -1, keepdims=True) + e_n
        den = jnp.where(den > 0, den, 1.0)
        return _dot_nt((e / den).astype(BF), vt) + (e_n / den).astype(BF).astype(F32) * v_n.astype(BF).astype(F32)

    outs = []
    for g in range(NSA_G):
        q4 = _sample_q4(q, g).astype(BF)
        q4f = q4.astype(F32)
        kts, vts = [], []
        idxv = jnp.zeros((1, nkeys), jnp.int32)
        any_new = False
        for k in range(N_TOP):
            ik = idx_ref[(b * NSA_G + g) * N_TOP + k]
            kts.append(blks[g * N_TOP + k][0, 0, 0, 0])
            vts.append(blks[g * N_TOP + k][0, 0, 1, 0])
            idxv = jnp.where(kslot == k, ik, idxv)
            any_new = jnp.logical_or(any_new, ik >= n_past_blk)
        kmat = jnp.concatenate(kts, axis=1).astype(BF)
        vmat = jnp.concatenate(vts, axis=1).astype(BF)
        pi = jnp.clip(idxv, 0, n_past_blk - 1)
        in_half = (prow // L_SEL) == (pi % sub_pp)
        pos = idxv * L_SEL + prow % L_SEL
        mask = (idxv >= 0) & (idxv < n_past_blk) & in_half & (pos <= t_pos)
        s = jnp.where(mask, _dot(q4, kmat), -jnp.inf)
        c0 = NSA_ROW + g * NSA_DH
        new_k = newkv[:, c0:c0 + NSA_DH]
        new_v = newkv[:, c0 + NSA_G * NSA_DH:c0 + NSA_G * NSA_DH + NSA_DH]
        s_n = jnp.sum(q4f * new_k.astype(BF).astype(F32), -1, keepdims=True)
        s_n = jnp.where(jnp.logical_and(any_new, new_visible), s_n, -jnp.inf)
        o_slc = with_new_key(s, s_n, vmat, new_v)
        c1 = 2 * NSA_ROW + g * NSA_DH
        nwk = newkv[:, c1:c1 + NSA_DH]
        nwv = newkv[:, c1 + NSA_G * NSA_DH:c1 + NSA_G * NSA_DH + NSA_DH]
        s_w = jnp.where(m_w, _dot(q4, win_ref[0, 0, 0, g].astype(BF)), -jnp.inf)
        s_wn = jnp.sum(q4f * nwk.astype(BF).astype(F32), -1, keepdims=True)
        o_win = with_new_key(s_w, s_wn, win_ref[0, 0, 1, g].astype(BF), nwv)
        for hp in range(NSA_HPG):
            hcol = slice((g * NSA_HPG + hp) * NSA_DH, (g * NSA_HPG + hp + 1) * NSA_DH)
            outs.append(_head_gate(gates, g, hp, 0) * ocmp[:, hcol] + _head_gate(gates, g, hp, 1) * o_slc[hp:hp + 1]
                        + _head_gate(gates, g, hp, 2) * o_win[hp:hp + 1])
    o_ref[0] = jnp.concatenate(outs, axis=1).astype(BF)


def _sample_attn(idx_flat, pt_flat, slc_t, u_s3, win_t, ocmp, layer, db, n_pages, t_pos, l_buf):
    sub_pp = PAGE_SIZE // L_SEL
    n_past_blk = n_pages * sub_pp

    def blk_map(g, k):
        def f(b, idx, pt):
            pi = jnp.clip(idx[(b * NSA_G + g) * N_TOP + k], 0, n_past_blk - 1)
            return (layer, pt[b * n_pages + pi // sub_pp], 0, g, 0, 0)
        return f

    fix3 = lambda col: (lambda b, idx, pt: (b, 0, col))
    grid_spec = pltpu.PrefetchScalarGridSpec(
        num_scalar_prefetch=2,
        grid=(db,),
        in_specs=[pl.BlockSpec((1, 1, 2, 1, NSA_DH, PAGE_SIZE), blk_map(g, k))
                  for g in range(NSA_G) for k in range(N_TOP)]
        + [pl.BlockSpec((1, 1, NSA_W), fix3(P_NQ // NSA_W)),
           pl.BlockSpec((1, 1, 128), fix3(P_SMALL // 128)),
           pl.BlockSpec((1, 1, NSA_KV), fix3(P_NKV // NSA_KV)),
           pl.BlockSpec((1, 1, 2, NSA_G, NSA_DH, l_buf), lambda b, idx, pt: (layer, b, 0, 0, 0, 0)),
           pl.BlockSpec((1, 1, NSA_W), fix3(0))],
        out_specs=pl.BlockSpec((1, 1, NSA_W), fix3(0)),
    )
    return pl.pallas_call(
        functools.partial(_sample_attn_kernel, t_pos=t_pos, n_past_blk=n_past_blk, l_buf=l_buf),
        grid_spec=grid_spec,
        out_shape=jax.ShapeDtypeStruct((db, 1, NSA_W), BF),
        compiler_params=_params("arbitrary"),
        name="nsa_sample_attn",
    )(idx_flat, pt_flat, *([slc_t] * (NSA_G * N_TOP)), u_s3, u_s3, u_s3, win_t, ocmp)


def _pack_w_in(w):
    zeros = jnp.zeros((w.shape[0], 128 - 2 * DN_H - 3 * NSA_H + 128), w.dtype)
    return jnp.concatenate(
        [w[:, OFF_MERGE:N_IN], w[:, 0:DN_CONV_C], w[:, OFF_NSA_KV:OFF_NSA_G], w[:, OFF_DN_B:OFF_CF],
         w[:, OFF_NSA_G:OFF_MERGE], zeros, w[:, OFF_DN_Z:OFF_DN_B], w[:, OFF_CF:OFF_NSA_Q],
         w[:, OFF_NSA_Q:OFF_NSA_KV]], axis=1).astype(BF)


def _pack_cmp_weights(pe, w1, w2):
    eye = jnp.eye(2, dtype=F32)
    w1r = w1.reshape(2, L_CMP, NSA_DH, NSA_DH)
    w1big = jnp.einsum("slde,sp,gq->lsgdpqe", w1r, eye, eye).reshape(CMP_FLAT, NSA_ROW).astype(BF)
    w2big = jnp.einsum("sde,sp,gq->sgdpqe", w2, eye, eye).reshape(NSA_ROW, NSA_ROW).astype(BF)
    pe_flat = jnp.broadcast_to(pe.transpose(1, 0, 2)[:, :, None, :], (L_CMP, 2, NSA_G, NSA_DH)).reshape(1, CMP_FLAT)
    return pe_flat, w1big, w2big


def _pack_dn_par(a_log, dt_bias):
    par = jnp.zeros((8, 128), F32)
    par = par.at[0, SM_A:SM_A + DN_H].set(a_log)
    return par.at[1, SM_A:SM_A + DN_H].set(dt_bias)


def _pack_ffn(w_in, w_out):
    fc = D_FF // FFN_CHUNKS
    wi = jnp.stack([jnp.concatenate([w_in[:, c * fc:(c + 1) * fc], w_in[:, D_FF + c * fc:D_FF + (c + 1) * fc]], axis=1)
                    for c in range(FFN_CHUNKS)]).astype(BF)
    wo = w_out.reshape(FFN_CHUNKS, fc, D_MODEL).astype(BF)
    return wi, wo


def kernel(x_prompt, x_sample, cache_nsa_cmp, cache_nsa_slc, cache_nsa_win, state_dn, state_dn_conv, state_cf_conv, page_table, w_in, dn_conv_w, dn_a_log, dn_dt_bias, dn_norm_g, cf_dw_w, cf_dw_b, cf_ln_g, cf_ln_b, nsa_cmp_pe, nsa_cmp_w1, nsa_cmp_w2, w_branch_a, w_branch_b, w_branch_c, w_out, ln1_g, ln1_b, ln2_g, ln2_b, w_ffn_in, w_ffn_out):
    bsz, t, d = x_prompt.shape
    db = x_sample.shape[0]
    depth = w_in.shape[0]
    n_pool = cache_nsa_cmp.shape[1]
    n_pages = page_table.shape[1]
    past = n_pages * PAGE_SIZE
    l_buf = cache_nsa_win.shape[2]
    alpha = _alpha(depth)
    assert x_sample.shape[1] == 1 and d == D_MODEL
    assert t % CF_TILE == 0 and t >= WINDOW + Q_BLOCK and t % SEL_TK == 0 and n_pages % CMP_PAGES == 0

    cmp4 = cache_nsa_cmp.reshape(depth, n_pool, BLK_PER_PAGE, CMP_FLAT)
    slc_t = jnp.transpose(cache_nsa_slc, (0, 1, 3, 4, 5, 2))
    win_t = jnp.transpose(cache_nsa_win, (0, 1, 3, 4, 5, 2))
    pt_flat = page_table.reshape(-1)
    tm_p = 512 if (bsz * t) % 512 == 0 else 256
    wlen = min(WINDOW, t)

    xp = x_prompt.reshape(bsz * t, d)
    xs = x_sample.reshape(db, d)
    st = {k: [] for k in ("cmp_p", "cmp_s", "slc_p", "slc_s", "win_p", "win_s", "dn_p", "dn_s", "dnc_p", "dnc_s",
                          "cfc_p", "cfc_s")}
    for l in range(depth):
        wr = _pack_w_in(w_in[l])
        w_qkv_t = w_in[l][:, OFF_NSA_Q:OFF_NSA_G].T.astype(BF)
        pe_flat, w1big, w2big = _pack_cmp_weights(nsa_cmp_pe[l], nsa_cmp_w1[l], nsa_cmp_w2[l])
        par = _pack_dn_par(dn_a_log[l], dn_dt_bias[l])
        ng = dn_norm_g[l].reshape(1, DN_DV)
        fbias, fg, fb = cf_dw_b[l].reshape(1, CF_C), cf_ln_g[l].reshape(1, CF_C), cf_ln_b[l].reshape(1, CF_C)
        pa, pb, pc, wo = (w_branch_a[l].astype(BF), w_branch_b[l].astype(BF), w_branch_c[l].astype(BF),
                          w_out[l].astype(BF))
        wi_f, wo_f = _pack_ffn(w_ffn_in[l], w_ffn_out[l])
        g1, b1, g2, b2 = (ln1_g[l].reshape(1, d), ln1_b[l].reshape(1, d), ln2_g[l].reshape(1, d),
                          ln2_b[l].reshape(1, d))

        u_p = _matmul(xp, wr, tm_p, 1536)
        u_p3 = u_p.reshape(bsz, t, N_PAD)
        qkvt_bf, kvt = _proj_nt(xp.reshape(bsz, t, d), w_qkv_t, 512)
        oa, dn_state = _dn_prompt(u_p3, dn_conv_w[l], par, ng)
        ob, cf_tail = _cf_prompt(u_p, cf_dw_w[l], fbias, fg, fb, bsz, t)
        kvc, kvct = _compress_prompt(u_p[:, P_NKV:P_NKV + NSA_ROW].reshape(bsz, t // L_CMP, CMP_FLAT), pe_flat, w1big,
                                     w2big)
        kv_bf = u_p3[:, :, P_NKV:P_NKV + NSA_KV].astype(BF)
        oc = _nsa_prompt(u_p, qkvt_bf, kvc, kvct, kv_bf, bsz, t)
        x1 = _merge(xp, oa.reshape(bsz * t, DN_W), ob, oc, u_p, pa, pb, pc, wo, g1, b1, 256, alpha)
        xp_next = _ffn(x1, wi_f, wo_f, g2, b2, tm_p, alpha)
        kvt6 = kvt.reshape(bsz, 3, 2, NSA_G, NSA_DH, t)
        st["cmp_p"].append(jnp.transpose(kvt6[:, 0], (0, 4, 1, 2, 3)))
        st["slc_p"].append(jnp.transpose(kvt6[:, 1], (0, 4, 1, 2, 3)))
        st["win_p"].append(jnp.transpose(kvt6[:, 2, :, :, :, t - wlen:], (0, 4, 1, 2, 3)))
        st["dn_p"].append(dn_state)
        st["dnc_p"].append(u_p3[:, t - (DN_CONV - 1):, P_DNQKV:P_DNQKV + DN_CONV_C])
        st["cfc_p"].append(cf_tail[:, CF_HALO - (CF_K - 1):])

        u_s = _matmul(xs, wr, db, 1536)
        u_s3 = u_s.reshape(db, 1, N_PAD)
        oa_s, ob_s, dn_state_s, glu_s = _sample_mix(u_s3, state_dn, state_dn_conv, state_cf_conv, l, dn_conv_w[l],
                                                    par, ng, cf_dw_w[l], fbias, fg, fb, db)
        kvc_s = _compress_sample(cmp4, pt_flat, l, pe_flat, w1big, w2big, db, n_pages)
        n_sel = n_pages * (PAGE_SIZE // L_SEL) + 1
        ocmp_s, idx = _sample_select(u_s3, kvc_s, db, past, n_sel)
        oc_s = _sample_attn(idx[:, :, :N_TOP].reshape(-1), pt_flat, slc_t, u_s3, win_t, ocmp_s, l, db, n_pages,
                            past, l_buf)
        x1s = _merge(xs, oa_s.reshape(db, DN_W), ob_s.reshape(db, CF_C), oc_s.reshape(db, NSA_W), u_s, pa, pb, pc,
                     wo, g1, b1, db, alpha)
        xs_next = _ffn(x1s, wi_f, wo_f, g2, b2, db, alpha)
        kv_s = u_s[:, P_NKV:P_NKV + NSA_KV].reshape(db, 1, 3, 2, NSA_G, NSA_DH)
        st["cmp_s"].append(kv_s[:, :, 0])
        st["slc_s"].append(kv_s[:, :, 1])
        st["win_s"].append(jnp.concatenate([cache_nsa_win[l], kv_s[:, :, 2]], axis=1)[:, -min(WINDOW, l_buf + 1):])
        st["dn_s"].append(dn_state_s)
        st["dnc_s"].append(jnp.concatenate([state_dn_conv[l], u_s[:, None, P_DNQKV:P_DNQKV + DN_CONV_C]],
                                           axis=1)[:, -(DN_CONV - 1):])
        st["cfc_s"].append(jnp.concatenate([state_cf_conv[l], glu_s], axis=1)[:, -(CF_K - 1):])
        xp, xs = xp_next, xs_next

    stk = {k: jnp.stack(v) for k, v in st.items()}
    return (xp.reshape(bsz, t, d), xs.reshape(db, 1, d), stk["cmp_p"], stk["cmp_s"], stk["slc_p"], stk["slc_s"],
            stk["win_p"], stk["win_s"], stk["dn_p"], stk["dn_s"], stk["dnc_p"], stk["dnc_s"], stk["cfc_p"],
            stk["cfc_s"])
```

```python
import functools

import jax
import jax.numpy as jnp
from jax import lax
from jax.experimental import pallas as pl
from jax.experimental.pallas import tpu as pltpu

F32 = jnp.float32
BF = jnp.bfloat16
HI = lax.Precision.HIGHEST

D_MODEL = 1024
DN_H = 4
DN_DK = 128
DN_DV = 128
DN_QK = DN_H * DN_DK
DN_W = DN_H * DN_DV
DN_CONV_C = 2 * DN_QK + DN_W
DN_CONV = 4
DN_CHUNK = 64
CF_C = 512
CF_K = 31
NSA_H = 8
NSA_G = 2
NSA_HPG = NSA_H // NSA_G
NSA_DH = 64
NSA_W = NSA_H * NSA_DH
NSA_KV = 3 * 2 * NSA_G * NSA_DH
NSA_ROW = 2 * NSA_G * NSA_DH
L_CMP = 32
L_SEL = 64
N_TOP = 16
WINDOW = 512
Q_BLOCK = 128
FORCE_SCORE = 1.0e4
PAGE_SIZE = 128
D_FF = 2816
FFN_CHUNKS = 2
NEG_BIG = -1.0e30

OFF_DN_Z = DN_CONV_C
OFF_DN_B = OFF_DN_Z + DN_W
OFF_DN_A = OFF_DN_B + DN_H
OFF_CF = OFF_DN_A + DN_H
OFF_NSA_Q = OFF_CF + 2 * CF_C
OFF_NSA_KV = OFF_NSA_Q + NSA_W
OFF_NSA_G = OFF_NSA_KV + NSA_KV
OFF_MERGE = OFF_NSA_G + 3 * NSA_H
N_IN = OFF_MERGE + 3 * D_MODEL

P_MERGE = 0
P_DNQKV = 3072
P_NKV = 4608
P_SMALL = 5376
P_DNZ = 5632
P_CF = 6144
P_NQ = 7168
N_PAD = 7680
SM_B = 0
SM_A = DN_H
SM_G = 2 * DN_H

def _alpha(depth):
    return (2 * depth) ** 0.25


def _dot(a, b, precision=None):
    return jnp.dot(a, b, preferred_element_type=F32, precision=precision)


def _dot_nt(a, b):
    return lax.dot_general(a, b, (((1,), (1,)), ((), ())), preferred_element_type=F32)


def _dot_tn(a, b):
    return lax.dot_general(a, b, (((0,), (0,)), ((), ())), preferred_element_type=F32)


def _silu(x):
    return x * jax.nn.sigmoid(x)


def _softplus(x):
    return jnp.maximum(x, 0.0) + jnp.log(1.0 + jnp.exp(-jnp.abs(x)))


def _l2norm(x):
    return x * lax.rsqrt(jnp.sum(x * x, -1, keepdims=True) + 1e-6)


def _layer_norm(x, g, b):
    xc = x - jnp.mean(x, -1, keepdims=True)
    var = jnp.mean(xc * xc, -1, keepdims=True)
    return xc * lax.rsqrt(var + 1e-5) * g + b


def _masked_softmax(s, mask):
    s = jnp.where(mask, s, -jnp.inf)
    m = jnp.max(s, -1, keepdims=True)
    m = jnp.where(m > -jnp.inf, m, 0.0)
    e = jnp.exp(s - m)
    den = jnp.sum(e, -1, keepdims=True)
    return e / jnp.where(den > 0, den, 1.0)


def _params(*sem, vmem_mb=None):
    kw = {}
    if vmem_mb is not None:
        kw["vmem_limit_bytes"] = vmem_mb * 1024 * 1024
    return pltpu.CompilerParams(dimension_semantics=sem, **kw)


def _mm_kernel(x_ref, w_ref, o_ref):
    o_ref[...] = _dot(x_ref[...].astype(BF), w_ref[...])


def _matmul(x, w, tm, tn):
    m, k = x.shape
    n = w.shape[1]
    return pl.pallas_call(
        _mm_kernel,
        grid=(m // tm, n // tn),
        in_specs=[pl.BlockSpec((tm, k), lambda i, j: (i, 0)),
                  pl.BlockSpec((k, tn), lambda i, j: (0, j))],
        out_specs=pl.BlockSpec((tm, tn), lambda i, j: (i, j)),
        out_shape=jax.ShapeDtypeStruct((m, n), F32),
        compiler_params=_params("arbitrary", "arbitrary"),
        name="proj_matmul",
    )(x, w)


def _dn_gates(sm, par):
    beta = jax.nn.sigmoid(sm)
    g = -jnp.exp(par[0:1, :]) * _softplus(sm + par[1:2, :])
    return beta, g


def _split2(x):
    hi = x.astype(BF)
    return hi, (x - hi.astype(F32)).astype(BF)


def _dot_split(a, b):
    return _dot(a[0], b[0]) + _dot(a[0], b[1]) + _dot(a[1], b[0])


def _dn_prep_kernel(qkv_ref, sm_ref, cw_ref, par_ref, w_ref, kc_ref, qd_ref, kd_ref, qk_ref, gl_ref, xp_ref):
    n = pl.program_id(0)
    nb = qkv_ref.shape[0]
    c = DN_CHUNK

    @pl.when(n == 0)
    def _():
        xp_ref[:, 0:8, :] = jnp.zeros((nb, 8, DN_CONV_C), F32)

    cw = cw_ref[...]
    par = par_ref[...]
    ii = lax.broadcasted_iota(jnp.int32, (c, c), 0)
    jj = lax.broadcasted_iota(jnp.int32, (c, c), 1)
    tri = jnp.where(ii >= jj, 1.0, 0.0).astype(BF)
    eye = jnp.where(ii == jj, 1.0, 0.0).astype(F32)
    ch = []
    for bi in range(nb):
        raw = qkv_ref[bi]
        xp_ref[bi, 8:8 + c, :] = raw
        conv = cw[0:1, :] * xp_ref[bi, 5:5 + c, :]
        for k in range(1, DN_CONV):
            conv = conv + cw[k:k + 1, :] * xp_ref[bi, 5 + k:5 + k + c, :]
        xp_ref[bi, 0:8, :] = raw[c - 8:c, :]
        h = _silu(conv)
        beta_all, g_all = _dn_gates(sm_ref[bi], par)
        g1 = g_all.astype(BF)
        r1 = g_all - g1.astype(F32)
        g2 = r1.astype(BF)
        g3 = (r1 - g2.astype(F32)).astype(BF)
        gcum = _dot(tri, g1) + _dot(tri, g2) + _dot(tri, g3)
        gcum_t = gcum.T
        gl_ref[0, bi:bi + 1, :] = jnp.exp(gcum[c - 1:c, :])
        for hd in range(DN_H):
            qh = _l2norm(h[:, hd * DN_DK:(hd + 1) * DN_DK]) * (DN_DK ** -0.5)
            kh = _l2norm(h[:, DN_QK + hd * DN_DK:DN_QK + (hd + 1) * DN_DK])
            vh = h[:, 2 * DN_QK + hd * DN_DV:2 * DN_QK + (hd + 1) * DN_DV]
            bcol = beta_all[:, SM_B + hd:SM_B + hd + 1]
            gc = gcum[:, SM_A + hd:SM_A + hd + 1]
            gr = gcum_t[SM_A + hd:SM_A + hd + 1, :]
            dec_incl = jnp.exp(jnp.where(ii >= jj, gc - gr, -jnp.inf))
            ch.append(dict(bi=bi, hd=hd, qh=qh, kh=kh, vh=vh, bcol=bcol, gc=gc, dec_incl=dec_incl))
    for s in ch:
        kb = s["kh"].astype(BF)
        both = _dot_nt(jnp.concatenate([s["qh"].astype(BF), kb], axis=0), kb)
        s["qk"] = both[:c] * s["dec_incl"]
        nm = -(s["bcol"] * both[c:] * jnp.where(ii > jj, s["dec_incl"], 0.0))
        s["inv"] = eye + nm
        s["pw"] = _split2(nm)
    for _ in range(5):
        for s in ch:
            s["pw"] = _split2(_dot_split(s["pw"], s["pw"]))
        for s in ch:
            s["inv"] = s["inv"] + _dot_split(_split2(s["inv"]), s["pw"])
    for s in ch:
        eg = jnp.exp(s["gc"])
        rhs = jnp.concatenate([s["bcol"] * s["vh"], (s["bcol"] * eg) * s["kh"]], axis=1)
        xsol = _dot_split(_split2(s["inv"]), _split2(rhs))
        bi = s["bi"]
        col = slice(s["hd"] * DN_DV, (s["hd"] + 1) * DN_DV)
        w_ref[bi, :, col] = xsol[:, :DN_DV]
        kc_ref[bi, :, col] = xsol[:, DN_DV:].astype(BF)
        qd_ref[bi, :, col] = (s["qh"] * eg).astype(BF)
        kd_ref[bi, :, col] = (s["kh"] * jnp.exp(s["gc"][c - 1:c, :] - s["gc"])).astype(BF)
        qk_ref[bi, :, col] = jnp.concatenate([s["qk"], jnp.zeros((c, DN_DV - c), F32)], axis=1).astype(BF)


def _dn_scan_kernel(w_ref, kc_ref, qd_ref, kd_ref, qk_ref, gl_ref, z_ref, ng_ref, oa_ref, st_ref, s_ref, *, n_chunks):
    n = pl.program_id(0)
    c = DN_CHUNK

    @pl.when(n == 0)
    def _():
        s_ref[...] = jnp.zeros_like(s_ref)

    ng = ng_ref[...]
    for bi in range(w_ref.shape[0]):
        g_last = gl_ref[0, bi:bi + 1, :]
        z = z_ref[bi]
        outs = []
        for hd in range(DN_H):
            col = slice(hd * DN_DV, (hd + 1) * DN_DV)
            s_old = s_ref[bi, hd]
            sb = s_old.astype(BF)
            u = w_ref[bi, :, col] - _dot(kc_ref[bi, :, col], sb)
            ub = u.astype(BF)
            o = _dot(qd_ref[bi, :, col], sb) + _dot(qk_ref[bi, :, hd * DN_DV:hd * DN_DV + c], ub)
            s_ref[bi, hd] = g_last[:, SM_A + hd:SM_A + hd + 1] * s_old + _dot_tn(kd_ref[bi, :, col], ub)
            o = o * lax.rsqrt(jnp.mean(o * o, -1, keepdims=True) + 1e-6) * ng
            outs.append(o * _silu(z[:, col]))
        oa_ref[bi] = jnp.concatenate(outs, axis=1).astype(BF)

    @pl.when(n == n_chunks - 1)
    def _():
        st_ref[...] = s_ref[...]


def _dn_prompt(u_p3, conv_w, par, ng):
    b, t, _ = u_p3.shape
    nc = t // DN_CHUNK
    c = DN_CHUNK
    row = lambda n: (0, n, 0)
    fix = lambda n: (0, 0)
    w_val, k_cum, q_dec, k_dec, qk, g_last = pl.pallas_call(
        _dn_prep_kernel,
        grid=(nc,),
        in_specs=[pl.BlockSpec((b, c, DN_CONV_C), lambda n: (0, n, P_DNQKV // DN_CONV_C)),
                  pl.BlockSpec((b, c, 128), lambda n: (0, n, P_SMALL // 128)),
                  pl.BlockSpec((DN_CONV, DN_CONV_C), fix),
                  pl.BlockSpec((8, 128), fix)],
        out_specs=[pl.BlockSpec((b, c, DN_W), row)] * 5 + [pl.BlockSpec((1, b, 128), lambda n: (n, 0, 0))],
        out_shape=[jax.ShapeDtypeStruct((b, t, DN_W), F32)] + [jax.ShapeDtypeStruct((b, t, DN_W), BF)] * 4
        + [jax.ShapeDtypeStruct((nc, b, 128), F32)],
        scratch_shapes=[pltpu.VMEM((b, 8 + c, DN_CONV_C), F32)],
        compiler_params=_params("arbitrary", vmem_mb=48),
        name="dn_prep",
    )(u_p3, u_p3, conv_w, par)
    return pl.pallas_call(
        functools.partial(_dn_scan_kernel, n_chunks=nc),
        grid=(nc,),
        in_specs=[pl.BlockSpec((b, c, DN_W), row)] * 5
        + [pl.BlockSpec((1, b, 128), lambda n: (n, 0, 0)),
           pl.BlockSpec((b, c, DN_W), lambda n: (0, n, P_DNZ // DN_W)),
           pl.BlockSpec((1, DN_DV), fix)],
        out_specs=[pl.BlockSpec((b, c, DN_W), row),
                   pl.BlockSpec((b, DN_H, DN_DK, DN_DV), lambda n: (0, 0, 0, 0))],
        out_shape=[jax.ShapeDtypeStruct((b, t, DN_W), BF),
                   jax.ShapeDtypeStruct((b, DN_H, DN_DK, DN_DV), F32)],
        scratch_shapes=[pltpu.VMEM((b, DN_H, DN_DK, DN_DV), F32)],
        compiler_params=_params("arbitrary"),
        name="dn_scan",
    )(w_val, k_cum, q_dec, k_dec, qk, g_last, u_p3, ng)


CF_TILE = 256
CF_ROWS = 32
CF_HALO = 32


def _cf_prompt_kernel(u_ref, w_ref, bias_ref, g_ref, b_ref, ob_ref, tail_ref, buf_ref, *, n_tiles):
    i = pl.program_id(1)
    tc = CF_TILE

    @pl.when(i == 0)
    def _():
        buf_ref[0:CF_HALO, :] = jnp.zeros((CF_HALO, CF_C), F32)

    u = u_ref[...]
    buf_ref[CF_HALO:CF_HALO + tc, :] = u[:, :CF_C] * jax.nn.sigmoid(u[:, CF_C:])
    bias = bias_ref[...]
    g = g_ref[...]
    b = b_ref[...]
    off = CF_HALO - (CF_K - 1)
    for r0 in range(0, tc, CF_ROWS):
        acc = bias + w_ref[0:1, :] * buf_ref[r0 + off:r0 + off + CF_ROWS, :]
        for k in range(1, CF_K):
            acc = acc + w_ref[k:k + 1, :] * buf_ref[r0 + off + k:r0 + off + k + CF_ROWS, :]
        ob_ref[r0:r0 + CF_ROWS, :] = _silu(_layer_norm(acc, g, b)).astype(BF)
    tail = buf_ref[tc:tc + CF_HALO, :]
    buf_ref[0:CF_HALO, :] = tail

    @pl.when(i == n_tiles - 1)
    def _():
        tail_ref[0] = tail


def _cf_prompt(u_p, w, bias, g, bb, b, t):
    nt = t // CF_TILE
    return pl.pallas_call(
        functools.partial(_cf_prompt_kernel, n_tiles=nt),
        grid=(b, nt),
        in_specs=[pl.BlockSpec((CF_TILE, 2 * CF_C), lambda i, n: (i * nt + n, P_CF // (2 * CF_C))),
                  pl.BlockSpec((CF_K, CF_C), lambda i, n: (0, 0)),
                  pl.BlockSpec((1, CF_C), lambda i, n: (0, 0)),
                  pl.BlockSpec((1, CF_C), lambda i, n: (0, 0)),
                  pl.BlockSpec((1, CF_C), lambda i, n: (0, 0))],
        out_specs=[pl.BlockSpec((CF_TILE, CF_C), lambda i, n: (i * nt + n, 0)),
                   pl.BlockSpec((1, CF_HALO, CF_C), lambda i, n: (i, 0, 0))],
        out_shape=[jax.ShapeDtypeStruct((b * t, CF_C), BF),
                   jax.ShapeDtypeStruct((b, CF_HALO, CF_C), F32)],
        scratch_shapes=[pltpu.VMEM((CF_HALO + CF_TILE, CF_C), F32)],
        compiler_params=_params("arbitrary", "arbitrary"),
        name="cf_prompt",
    )(u_p, w, bias, g, bb)


CMP_FLAT = L_CMP * NSA_ROW


def _cmp_mlp(x, pe, w1, w2):
    xb = (x + pe).astype(BF)
    h = _silu(_dot(xb, w1))
    return _dot(h.astype(BF), w2)


def _cmp_kernel(x_ref, pe_ref, w1_ref, w2_ref, o_ref, ot_ref):
    kvc = _cmp_mlp(x_ref[0], pe_ref[...], w1_ref[...], w2_ref[...])
    o_ref[0] = kvc
    ot_ref[0] = kvc.T


def _compress_prompt(x, pe, w1, w2):
    b, nc, _ = x.shape
    return pl.pallas_call(
        _cmp_kernel,
        grid=(b,),
        in_specs=[pl.BlockSpec((1, nc, CMP_FLAT), lambda i: (i, 0, 0)),
                  pl.BlockSpec((1, CMP_FLAT), lambda i: (0, 0)),
                  pl.BlockSpec((CMP_FLAT, NSA_ROW), lambda i: (0, 0)),
                  pl.BlockSpec((NSA_ROW, NSA_ROW), lambda i: (0, 0))],
        out_specs=[pl.BlockSpec((1, nc, NSA_ROW), lambda i: (i, 0, 0)),
                   pl.BlockSpec((1, NSA_ROW, nc), lambda i: (i, 0, 0))],
        out_shape=[jax.ShapeDtypeStruct((b, nc, NSA_ROW), F32),
                   jax.ShapeDtypeStruct((b, NSA_ROW, nc), F32)],
        compiler_params=_params("arbitrary"),
        name="nsa_compress_prompt",
    )(x, pe, w1, w2)


NT_ROWS = NSA_W + NSA_KV


def _proj_nt_kernel(x_ref, w_ref, obf_ref, okv_ref):
    r = _dot_nt(w_ref[...], x_ref[0].astype(BF))
    obf_ref[0] = r.astype(BF)
    okv_ref[0] = r[NSA_W:, :]


def _proj_nt(x3, w_t, tm):
    b, t, d = x3.shape
    return pl.pallas_call(
        _proj_nt_kernel,
        grid=(b, t // tm),
        in_specs=[pl.BlockSpec((1, tm, d), lambda i, j: (i, j, 0)),
                  pl.BlockSpec((NT_ROWS, d), lambda i, j: (0, 0))],
        out_specs=[pl.BlockSpec((1, NT_ROWS, tm), lambda i, j: (i, 0, j)),
                   pl.BlockSpec((1, NSA_KV, tm), lambda i, j: (i, 0, j))],
        out_shape=[jax.ShapeDtypeStruct((b, NT_ROWS, t), BF),
                   jax.ShapeDtypeStruct((b, NSA_KV, t), F32)],
        compiler_params=_params("arbitrary", "arbitrary"),
        name="proj_qkv_t",
    )(x3, w_t)


CMP_PAGES = 64
BLK_PER_PAGE = PAGE_SIZE // L_CMP


def _cmp_sample_kernel(pt_ref, cmp_hbm, pet_ref, w1_ref, w2_ref, o_ref, buf_ref, xs_ref, sem, *, layer, n_steps,
                       n_pg):
    s = pl.program_id(0)
    slot = lax.rem(s, 2)

    def page_copy(step, j, into):
        return pltpu.make_async_copy(cmp_hbm.at[layer, pt_ref[step * n_pg + j]], buf_ref.at[into, j], sem.at[into])

    @pl.when(s == 0)
    def _():
        for j in range(n_pg):
            page_copy(0, j, 0).start()

    @pl.when(s + 1 < n_steps)
    def _():
        for j in range(n_pg):
            page_copy(s + 1, j, 1 - slot).start()

    for j in range(n_pg):
        page_copy(s, j, slot).wait()

    pet = pet_ref[...]
    half = NSA_ROW // 2
    pair_rows = 2 * PAGE_SIZE
    grp = 2 * BLK_PER_PAGE
    r_out = lax.broadcasted_iota(jnp.int32, (pair_rows, pair_rows), 0)
    r_in = lax.broadcasted_iota(jnp.int32, (pair_rows, pair_rows), 1)
    src = ((r_out // BLK_PER_PAGE) % 2) * PAGE_SIZE + (r_out % BLK_PER_PAGE) * L_CMP + r_out // grp
    perm = jnp.where(r_in == src, 1.0, 0.0).astype(BF)

    def to_rows(jp, carry):
        x2 = jnp.concatenate([buf_ref[slot, 2 * jp] + pet, buf_ref[slot, 2 * jp + 1] + pet], axis=1).astype(BF)
        xt = _dot_nt(perm, x2)
        r0 = pl.multiple_of(jp * grp, grp)
        for l in range(L_CMP):
            xs_ref[0, l, pl.ds(r0, grp), :] = xt[l * grp:(l + 1) * grp, :half]
            xs_ref[1, l, pl.ds(r0, grp), :] = xt[l * grp:(l + 1) * grp, half:]
        return carry

    lax.fori_loop(0, n_pg // 2, to_rows, 0, unroll=2)
    nblk = n_pg * BLK_PER_PAGE
    acc = [jnp.zeros((nblk, half), F32), jnp.zeros((nblk, half), F32)]
    for l in range(L_CMP):
        for hf in range(2):
            a = xs_ref[hf, l].astype(BF)
            acc[hf] = acc[hf] + _dot(a, w1_ref[l, hf * half:(hf + 1) * half, hf * half:(hf + 1) * half])
    o_ref[0] = _dot(_silu(jnp.concatenate(acc, axis=1)).astype(BF), w2_ref[...])


def _compress_sample(cmp_t, pt_flat, layer, pe_t, w1r, w2, db, n_pages):
    n_pg = min(CMP_PAGES, n_pages)
    n_steps = db * n_pages // n_pg
    nblk = n_pg * BLK_PER_PAGE
    grid_spec = pltpu.PrefetchScalarGridSpec(
        num_scalar_prefetch=1,
        grid=(n_steps,),
        in_specs=[pl.BlockSpec(memory_space=pl.ANY),
                  pl.BlockSpec((NSA_ROW, PAGE_SIZE), lambda s, pt: (0, 0)),
                  pl.BlockSpec((L_CMP, NSA_ROW, NSA_ROW), lambda s, pt: (0, 0, 0)),
                  pl.BlockSpec((NSA_ROW, NSA_ROW), lambda s, pt: (0, 0))],
        out_specs=pl.BlockSpec((1, nblk, NSA_ROW), lambda s, pt: (s, 0, 0)),
        scratch_shapes=[pltpu.VMEM((2, n_pg, NSA_ROW, PAGE_SIZE), F32),
                        pltpu.VMEM((2, L_CMP, nblk, NSA_ROW // 2), F32),
                        pltpu.SemaphoreType.DMA((2,))],
    )
    out = pl.pallas_call(
        functools.partial(_cmp_sample_kernel, layer=layer, n_steps=n_steps, n_pg=n_pg),
        grid_spec=grid_spec,
        out_shape=jax.ShapeDtypeStruct((n_steps, nblk, NSA_ROW), F32),
        compiler_params=_params("arbitrary", vmem_mb=48),
        name="nsa_compress_sample",
    )(pt_flat, cmp_t, pe_t, w1r, w2)
    return out.reshape(db, n_pages * BLK_PER_PAGE, NSA_ROW)


SEL_TK = 512


def _pair_matrix(nc, n_sel_pad):
    ci = lax.broadcasted_iota(jnp.int32, (nc, n_sel_pad), 0)
    si = lax.broadcasted_iota(jnp.int32, (nc, n_sel_pad), 1)
    return jnp.where((ci >> 1) == si, 1.0, 0.0).astype(F32)


def _head_gate(gates, g, hp, br):
    col = SM_G + (g * NSA_HPG + hp) * 3 + br
    return gates[:, col:col + 1]


def _masked_softmax0(s, mask):
    s = jnp.where(mask, s, -jnp.inf)
    m = jnp.max(s, 0, keepdims=True)
    m = jnp.where(m > -jnp.inf, m, 0.0)
    e = jnp.exp(s - m)
    den = jnp.sum(e, 0, keepdims=True)
    return e / jnp.where(den > 0, den, 1.0)


def _lane_tile(x, n):
    return jnp.concatenate([x] * n, axis=1)


def _nsa_prompt_kernel(qt_ref, sm_ref, kvc_ref, kvct_ref, ks_ref, kw_ref, vst_ref, vwt_ref, o_ref, sel_ref, *, t_len):
    i = pl.program_id(1)
    qb = Q_BLOCK
    cols = NSA_HPG * qb
    start = i * qb
    nc = t_len // L_CMP
    n_sel = t_len // L_SEL
    blk_per_tile = SEL_TK // L_SEL
    qt = qt_ref[0] * (NSA_DH ** -0.5)
    gates_t = jax.nn.sigmoid(sm_ref[...]).T
    t1 = start + lax.broadcasted_iota(jnp.int32, (1, qb), 1)
    t4 = start + (lax.broadcasted_iota(jnp.int32, (1, cols), 1) & (qb - 1))
    q4 = [jnp.concatenate([qt[(g * NSA_HPG + hp) * NSA_DH:(g * NSA_HPG + hp + 1) * NSA_DH, :]
                           for hp in range(NSA_HPG)], axis=1) for g in range(NSA_G)]
    zq = jnp.zeros((NSA_DH, cols), BF)
    qbd = jnp.concatenate([jnp.concatenate([q4[0], zq], axis=1), jnp.concatenate([zq, q4[1]], axis=1)], axis=0)

    kvc = kvc_ref[0]
    kvct = kvct_ref[0]
    s_c = _dot(kvc[:, 0:NSA_G * NSA_DH].astype(BF), qbd)
    c_end = (lax.broadcasted_iota(jnp.int32, (nc, 1), 0) + 1) * L_CMP - 1
    m_c = c_end <= t4
    pair_t = jnp.where((lax.broadcasted_iota(jnp.int32, (n_sel, nc), 1) >> 1)
                       == lax.broadcasted_iota(jnp.int32, (n_sel, nc), 0), 1.0, 0.0).astype(F32)
    blk2 = lax.broadcasted_iota(jnp.int32, (n_sel, qb), 0)
    cur = t1 >> 6
    valid = blk2 * L_SEL <= t1
    forced = (blk2 == 0) | (blk2 == cur) | (blk2 == cur - 1)
    o_cmp = []
    for g in range(NSA_G):
        p_c = _masked_softmax0(s_c[:, g * cols:(g + 1) * cols], m_c)
        vct = kvct[NSA_G * NSA_DH + g * NSA_DH:NSA_G * NSA_DH + (g + 1) * NSA_DH, :].astype(BF)
        o_cmp.append(_dot(vct, p_c.astype(BF)))
        imp_c = p_c[:, 0:qb] + p_c[:, qb:2 * qb] + p_c[:, 2 * qb:3 * qb] + p_c[:, 3 * qb:4 * qb]
        imp = _dot(pair_t, imp_c, precision=HI)
        score = jnp.where(valid, jnp.where(forced, FORCE_SCORE, imp), -jnp.inf)
        rank = jnp.zeros((n_sel, qb), F32)
        for r in range(n_sel):
            sr = score[r:r + 1, :]
            tie = jnp.where(blk2 > r, 1.0, 0.0)
            rank = rank + jnp.where(sr > score, 1.0, jnp.where(sr == score, tie, 0.0))
        sel_ref[g] = jnp.where(rank < N_TOP, jnp.where(score > -jnp.inf, 1.0, 0.0), 0.0)

    n_kt = (start + qb + SEL_TK - 1) // SEL_TK

    def body(j, carry):
        k0 = pl.multiple_of(j * SEL_TK, SEL_TK)
        b0 = pl.multiple_of(j * blk_per_tile, blk_per_tile)
        s_both = _dot(ks_ref[0, pl.ds(k0, SEL_TK), :], qbd)
        causal = (k0 + lax.broadcasted_iota(jnp.int32, (SEL_TK, 1), 0)) <= t1
        out = []
        for g in range(NSA_G):
            m_i, l_i, acc = carry[3 * g:3 * g + 3]
            st = sel_ref[g, pl.ds(b0, blk_per_tile), :]
            mexp = jnp.concatenate([jnp.broadcast_to(st[bb:bb + 1, :], (L_SEL, qb)) for bb in range(blk_per_tile)],
                                   axis=0)
            mask = _lane_tile(jnp.where(causal, mexp, 0.0), NSA_HPG) > 0.5
            s = jnp.where(mask, s_both[:, g * cols:(g + 1) * cols], NEG_BIG)
            m_new = jnp.maximum(m_i, jnp.max(s, 0, keepdims=True))
            alpha = jnp.exp(m_i - m_new)
            p = jnp.exp(s - m_new)
            l_new = alpha * l_i + jnp.sum(p, 0, keepdims=True)
            vt = vst_ref[0, g * NSA_DH:(g + 1) * NSA_DH, pl.ds(k0, SEL_TK)]
            out += [m_new, l_new, alpha * acc + _dot(vt, p.astype(BF))]
        return tuple(out)

    init = (jnp.full((1, cols), NEG_BIG, F32), jnp.zeros((1, cols), F32), jnp.zeros((NSA_DH, cols), F32)) * NSA_G
    res = lax.fori_loop(0, n_kt, body, init)
    o_slc = [res[3 * g + 2] / jnp.where(res[3 * g + 1] > 0, res[3 * g + 1], 1.0) for g in range(NSA_G)]

    kstart = pl.multiple_of(jnp.maximum(start - WINDOW, 0), qb)
    wlen = WINDOW + qb
    s_w = _dot(kw_ref[0, pl.ds(kstart, wlen), :], qbd)
    dist = t4 - (kstart + lax.broadcasted_iota(jnp.int32, (wlen, 1), 0))
    m_w = (dist >= 0) & (dist <= WINDOW)
    outs = []
    for g in range(NSA_G):
        p_w = _masked_softmax0(s_w[:, g * cols:(g + 1) * cols], m_w)
        o_win = _dot(vwt_ref[0, g * NSA_DH:(g + 1) * NSA_DH, pl.ds(kstart, wlen)], p_w.astype(BF))
        for hp in range(NSA_HPG):
            c = slice(hp * qb, (hp + 1) * qb)
            row = SM_G + (g * NSA_HPG + hp) * 3
            outs.append(gates_t[row:row + 1, :] * o_cmp[g][:, c] + gates_t[row + 1:row + 2, :] * o_slc[g][:, c]
                        + gates_t[row + 2:row + 3, :] * o_win[:, c])
    o_ref[...] = jnp.concatenate(outs, axis=0).T.astype(BF)


def _nsa_prompt(u_p, qkvt_bf, kvc, kvct, kv_bf, b, t):
    nq = t // Q_BLOCK
    nc = t // L_CMP
    k_blk = NSA_G * NSA_DH
    slc_k = NSA_ROW // k_blk
    win_k = 2 * NSA_ROW // k_blk
    slc_v = (NSA_W + NSA_ROW + k_blk) // k_blk
    win_v = (NSA_W + 2 * NSA_ROW + k_blk) // k_blk
    return pl.pallas_call(
        functools.partial(_nsa_prompt_kernel, t_len=t),
        grid=(b, nq),
        in_specs=[pl.BlockSpec((1, NSA_W, Q_BLOCK), lambda i, n: (i, 0, n)),
                  pl.BlockSpec((Q_BLOCK, 128), lambda i, n: (i * nq + n, P_SMALL // 128)),
                  pl.BlockSpec((1, nc, NSA_ROW), lambda i, n: (i, 0, 0)),
                  pl.BlockSpec((1, NSA_ROW, nc), lambda i, n: (i, 0, 0)),
                  pl.BlockSpec((1, t, k_blk), lambda i, n: (i, 0, slc_k)),
                  pl.BlockSpec((1, t, k_blk), lambda i, n: (i, 0, win_k)),
                  pl.BlockSpec((1, k_blk, t), lambda i, n: (i, slc_v, 0)),
                  pl.BlockSpec((1, k_blk, t), lambda i, n: (i, win_v, 0))],
        out_specs=pl.BlockSpec((Q_BLOCK, NSA_W), lambda i, n: (i * nq + n, 0)),
        out_shape=jax.ShapeDtypeStruct((b * t, NSA_W), BF),
        scratch_shapes=[pltpu.VMEM((NSA_G, t // L_SEL, Q_BLOCK), F32)],
        compiler_params=_params("arbitrary", "arbitrary", vmem_mb=48),
        name="nsa_prompt",
    )(qkvt_bf, u_p, kvc, kvct, kv_bf, kv_bf, qkvt_bf, qkvt_bf)


def _merge_kernel(x_ref, oa_ref, ob_ref, oc_ref, gm_ref, pa_ref, pb_ref, pc_ref, wo_ref, g_ref, b_ref, o_ref, *, alpha):
    gm = jax.nn.sigmoid(gm_ref[...])
    d = D_MODEL
    mix_in = (gm[:, 0:d] * _dot(oa_ref[...], pa_ref[...]) + gm[:, d:2 * d] * _dot(ob_ref[...], pb_ref[...])
              + gm[:, 2 * d:3 * d] * _dot(oc_ref[...], pc_ref[...]))
    mix = _dot(mix_in.astype(BF), wo_ref[...])
    o_ref[...] = _layer_norm(alpha * x_ref[...] + mix, g_ref[...], b_ref[...])


def _merge(x, oa, ob, oc, u, pa, pb, pc, wo, g, b, tm, alpha):
    m = x.shape[0]
    row = lambda i: (i, 0)
    fix = lambda i: (0, 0)
    return pl.pallas_call(
        functools.partial(_merge_kernel, alpha=alpha),
        grid=(m // tm,),
        in_specs=[pl.BlockSpec((tm, D_MODEL), row),
                  pl.BlockSpec((tm, DN_W), row), pl.BlockSpec((tm, CF_C), row), pl.BlockSpec((tm, NSA_W), row),
                  pl.BlockSpec((tm, 3 * D_MODEL), lambda i: (i, P_MERGE // (3 * D_MODEL))),
                  pl.BlockSpec((DN_W, D_MODEL), fix), pl.BlockSpec((CF_C, D_MODEL), fix),
                  pl.BlockSpec((NSA_W, D_MODEL), fix), pl.BlockSpec((D_MODEL, D_MODEL), fix),
                  pl.BlockSpec((1, D_MODEL), fix), pl.BlockSpec((1, D_MODEL), fix)],
        out_specs=pl.BlockSpec((tm, D_MODEL), row),
        out_shape=jax.ShapeDtypeStruct((m, D_MODEL), F32),
        compiler_params=_params("arbitrary", vmem_mb=48),
        name="merge_ln",
    )(x, oa, ob, oc, u, pa, pb, pc, wo, g, b)


def _ffn_kernel(x_ref, wi_ref, wo_ref, g_ref, b_ref, o_ref, acc_ref, *, alpha):
    j = pl.program_id(1)
    fc = D_FF // FFN_CHUNKS
    x = x_ref[...]
    h = _dot(x.astype(BF), wi_ref[0])
    y = _dot((_silu(h[:, :fc]) * h[:, fc:]).astype(BF), wo_ref[0])

    @pl.when(j == 0)
    def _():
        acc_ref[...] = y

    @pl.when(j > 0)
    def _():
        acc_ref[...] += y

    @pl.when(j == FFN_CHUNKS - 1)
    def _():
        o_ref[...] = _layer_norm(alpha * x + acc_ref[...], g_ref[...], b_ref[...])


def _ffn(x, wi, wo, g, b, tm, alpha):
    m = x.shape[0]
    fc = D_FF // FFN_CHUNKS
    return pl.pallas_call(
        functools.partial(_ffn_kernel, alpha=alpha),
        grid=(m // tm, FFN_CHUNKS),
        in_specs=[pl.BlockSpec((tm, D_MODEL), lambda i, j: (i, 0)),
                  pl.BlockSpec((1, D_MODEL, 2 * fc), lambda i, j: (j, 0, 0)),
                  pl.BlockSpec((1, fc, D_MODEL), lambda i, j: (j, 0, 0)),
                  pl.BlockSpec((1, D_MODEL), lambda i, j: (0, 0)),
                  pl.BlockSpec((1, D_MODEL), lambda i, j: (0, 0))],
        out_specs=pl.BlockSpec((tm, D_MODEL), lambda i, j: (i, 0)),
        out_shape=jax.ShapeDtypeStruct((m, D_MODEL), F32),
        scratch_shapes=[pltpu.VMEM((tm, D_MODEL), F32)],
        compiler_params=_params("arbitrary", "arbitrary", vmem_mb=48),
        name="ffn_ln",
    )(x, wi, wo, g, b)


def _sample_mix_kernel(qkv_ref, z_ref, sm_ref, cf_ref, st_ref, dcs_ref, cfs_ref, cw_ref, par_ref, ng_ref,
                       fw_ref, fbias_ref, fg_ref, fb_ref, oa_ref, ob_ref, so_ref, glu_ref):
    cs = dcs_ref[0, 0]
    new = qkv_ref[0]
    cw = cw_ref[...]
    conv = cw[DN_CONV - 1:DN_CONV, :] * new
    for k in range(DN_CONV - 1):
        conv = conv + cw[k:k + 1, :] * cs[k:k + 1, :]
    h = _silu(conv)
    beta_all, g_all = _dn_gates(sm_ref[0], par_ref[...])
    z = z_ref[0]
    ng = ng_ref[...]
    outs = []
    for hd in range(DN_H):
        qh = _l2norm(h[:, hd * DN_DK:(hd + 1) * DN_DK]) * (DN_DK ** -0.5)
        kh = _l2norm(h[:, DN_QK + hd * DN_DK:DN_QK + (hd + 1) * DN_DK])
        vh = h[:, 2 * DN_QK + hd * DN_DV:2 * DN_QK + (hd + 1) * DN_DV]
        beta = beta_all[:, SM_B + hd:SM_B + hd + 1]
        g = g_all[:, SM_A + hd:SM_A + hd + 1]
        kcol = jnp.broadcast_to(kh, (DN_DK, DN_DK)).T
        qcol = jnp.broadcast_to(qh, (DN_DK, DN_DK)).T
        s = jnp.exp(g) * st_ref[0, 0, hd]
        u = beta * (vh - jnp.sum(s * kcol, 0, keepdims=True))
        s = s + kcol * u
        so_ref[0, hd] = s
        o = jnp.sum(s * qcol, 0, keepdims=True)
        o = o * lax.rsqrt(jnp.mean(o * o, -1, keepdims=True) + 1e-6) * ng
        outs.append(o * _silu(z[:, hd * DN_DV:(hd + 1) * DN_DV]))
    oa_ref[0] = jnp.concatenate(outs, axis=1).astype(BF)

    ucf = cf_ref[0]
    glu = ucf[:, :CF_C] * jax.nn.sigmoid(ucf[:, CF_C:])
    glu_ref[0] = glu
    fw = fw_ref[...]
    acc = fbias_ref[...] + jnp.sum(fw[0:CF_K - 1, :] * cfs_ref[0, 0], 0, keepdims=True) + fw[CF_K - 1:CF_K, :] * glu
    ob_ref[0] = _silu(_layer_norm(acc, fg_ref[...], fb_ref[...])).astype(BF)


def _sample_mix(u_s3, state_dn, state_dn_conv, state_cf_conv, layer, conv_w, par, ng, fw, fbias, fg, fb, db):
    fix = lambda b: (0, 0)
    return pl.pallas_call(
        _sample_mix_kernel,
        grid=(db,),
        in_specs=[pl.BlockSpec((1, 1, DN_CONV_C), lambda b: (b, 0, P_DNQKV // DN_CONV_C)),
                  pl.BlockSpec((1, 1, DN_W), lambda b: (b, 0, P_DNZ // DN_W)),
                  pl.BlockSpec((1, 1, 128), lambda b: (b, 0, P_SMALL // 128)),
                  pl.BlockSpec((1, 1, 2 * CF_C), lambda b: (b, 0, P_CF // (2 * CF_C))),
                  pl.BlockSpec((1, 1, DN_H, DN_DK, DN_DV), lambda b: (layer, b, 0, 0, 0)),
                  pl.BlockSpec((1, 1, DN_CONV - 1, DN_CONV_C), lambda b: (layer, b, 0, 0)),
                  pl.BlockSpec((1, 1, CF_K - 1, CF_C), lambda b: (layer, b, 0, 0)),
                  pl.BlockSpec((DN_CONV, DN_CONV_C), fix),
                  pl.BlockSpec((8, 128), fix),
                  pl.BlockSpec((1, DN_DV), fix),
                  pl.BlockSpec((CF_K, CF_C), fix),
                  pl.BlockSpec((1, CF_C), fix), pl.BlockSpec((1, CF_C), fix), pl.BlockSpec((1, CF_C), fix)],
        out_specs=[pl.BlockSpec((1, 1, DN_W), lambda b: (b, 0, 0)),
                   pl.BlockSpec((1, 1, CF_C), lambda b: (b, 0, 0)),
                   pl.BlockSpec((1, DN_H, DN_DK, DN_DV), lambda b: (b, 0, 0, 0)),
                   pl.BlockSpec((1, 1, CF_C), lambda b: (b, 0, 0))],
        out_shape=[jax.ShapeDtypeStruct((db, 1, DN_W), BF),
                   jax.ShapeDtypeStruct((db, 1, CF_C), BF),
                   jax.ShapeDtypeStruct((db, DN_H, DN_DK, DN_DV), F32),
                   jax.ShapeDtypeStruct((db, 1, CF_C), F32)],
        compiler_params=_params("arbitrary"),
        name="sample_dn_cf",
    )(u_s3, u_s3, u_s3, u_s3, state_dn, state_dn_conv, state_cf_conv, conv_w, par, ng, fw, fbias, fg, fb)


def _sample_q4(q, g):
    rows = [q[:, (g * NSA_HPG + hp) * NSA_DH:(g * NSA_HPG + hp + 1) * NSA_DH] for hp in range(NSA_HPG)]
    return jnp.concatenate(rows + [jnp.zeros((8 - NSA_HPG, NSA_DH), F32)], axis=0)


def _sample_sel_kernel(q_ref, kvc_ref, ocmp_ref, idx_ref, *, t_pos, n_sel, n_sel_pad):
    q = q_ref[0] * (NSA_DH ** -0.5)
    kvc = kvc_ref[0]
    nc = kvc.shape[0]
    c_end = (lax.broadcasted_iota(jnp.int32, (1, nc), 1) + 1) * L_CMP - 1
    pair = _pair_matrix(nc, n_sel_pad)
    blk = lax.broadcasted_iota(jnp.int32, (1, n_sel_pad), 1)
    cur = t_pos // L_SEL
    valid = (blk < n_sel) & (blk * L_SEL <= t_pos)
    forced = (blk == 0) | (blk == cur) | (blk == cur - 1)
    lane = lax.broadcasted_iota(jnp.int32, (1, 128), 1)
    o_parts = []
    idx_rows = []
    for g in range(NSA_G):
        q4 = _sample_q4(q, g).astype(BF)
        kcol = slice(g * NSA_DH, (g + 1) * NSA_DH)
        vcol = slice(NSA_G * NSA_DH + g * NSA_DH, NSA_G * NSA_DH + (g + 1) * NSA_DH)
        s_c = _dot_nt(q4, kvc[:, kcol].astype(BF))
        p_c = _masked_softmax(s_c, c_end <= t_pos)
        o_cmp = _dot(p_c.astype(BF), kvc[:, vcol].astype(BF))
        imp_c = p_c[0:1] + p_c[1:2] + p_c[2:3] + p_c[3:4]
        imp = _dot(jnp.broadcast_to(imp_c, (8, nc)), pair, precision=HI)[0:1]
        score = jnp.where(valid, jnp.where(forced, FORCE_SCORE, imp), -jnp.inf)
        idx_vec = jnp.full((1, 128), -1, jnp.int32)
        for r in range(min(N_TOP, n_sel)):
            m = jnp.max(score, -1, keepdims=True)
            first = jnp.min(jnp.where(score == m, blk, n_sel_pad), -1, keepdims=True)
            idx_vec = jnp.where((lane == r) & (m > -jnp.inf), first, idx_vec)
            score = jnp.where(blk == first, -jnp.inf, score)
        idx_rows.append(idx_vec)
        o_parts += [o_cmp[hp:hp + 1, :] for hp in range(NSA_HPG)]
    ocmp_ref[0] = jnp.concatenate(o_parts, axis=1)
    idx_ref[0] = jnp.concatenate(idx_rows, axis=0)


def _sample_select(u_s3, kvc_s, db, t_pos, n_sel):
    nc = kvc_s.shape[1]
    n_sel_pad = -(-n_sel // 128) * 128
    return pl.pallas_call(
        functools.partial(_sample_sel_kernel, t_pos=t_pos, n_sel=n_sel, n_sel_pad=n_sel_pad),
        grid=(db,),
        in_specs=[pl.BlockSpec((1, 1, NSA_W), lambda b: (b, 0, P_NQ // NSA_W)),
                  pl.BlockSpec((1, nc, NSA_ROW), lambda b: (b, 0, 0))],
        out_specs=[pl.BlockSpec((1, 1, NSA_W), lambda b: (b, 0, 0)),
                   pl.BlockSpec((1, NSA_G, 128), lambda b: (b, 0, 0))],
        out_shape=[jax.ShapeDtypeStruct((db, 1, NSA_W), F32),
                   jax.ShapeDtypeStruct((db, NSA_G, 128), jnp.int32)],
        compiler_params=_params("arbitrary"),
        name="nsa_sample_select",
    )(u_s3, kvc_s)


def _sample_attn_kernel(idx_ref, pt_ref, *refs, t_pos, n_past_blk, l_buf):
    nblk = NSA_G * N_TOP
    blks = refs[:nblk]
    q_ref, sm_ref, nkv_ref, win_ref, ocmp_ref, o_ref = refs[nblk:]
    b = pl.program_id(0)
    q = q_ref[0] * (NSA_DH ** -0.5)
    gates = jax.nn.sigmoid(sm_ref[0])
    newkv = nkv_ref[0]
    ocmp = ocmp_ref[0]
    sub_pp = PAGE_SIZE // L_SEL
    nkeys = N_TOP * PAGE_SIZE
    lane = lax.broadcasted_iota(jnp.int32, (1, nkeys), 1)
    kslot = lane // PAGE_SIZE
    prow = lane % PAGE_SIZE
    new_visible = n_past_blk * L_SEL <= t_pos
    wpos = t_pos - l_buf + lax.broadcasted_iota(jnp.int32, (1, l_buf), 1)
    wdist = t_pos - wpos
    m_w = (wdist >= 0) & (wdist <= WINDOW) & (wpos >= 0)

    def with_new_key(s, s_n, vt, v_n):
        m = jnp.maximum(jnp.max(s, -1, keepdims=True), s_n)
        m = jnp.where(m > -jnp.inf, m, 0.0)
        e = jnp.exp(s - m)
        e_n = jnp.exp(s_n - m)
        den = jnp.sum(e, -1, keepdims=True) + e_n
        den = jnp.where(den > 0, den, 1.0)
        return _dot_nt((e / den).astype(BF), vt) + (e_n / den).astype(BF).astype(F32) * v_n.astype(BF).astype(F32)

    outs = []
    for g in range(NSA_G):
        q4 = _sample_q4(q, g).astype(BF)
        q4f = q4.astype(F32)
        kts, vts = [], []
        idxv = jnp.zeros((1, nkeys), jnp.int32)
        any_new = False
        for k in range(N_TOP):
            ik = idx_ref[(b * NSA_G + g) * N_TOP + k]
            kts.append(blks[g * N_TOP + k][0, 0, 0, 0])
            vts.append(blks[g * N_TOP + k][0, 0, 1, 0])
            idxv = jnp.where(kslot == k, ik, idxv)
            any_new = jnp.logical_or(any_new, ik >= n_past_blk)
        kmat = jnp.concatenate(kts, axis=1).astype(BF)
        vmat = jnp.concatenate(vts, axis=1).astype(BF)
        pi = jnp.clip(idxv, 0, n_past_blk - 1)
        in_half = (prow // L_SEL) == (pi % sub_pp)
        pos = idxv * L_SEL + prow % L_SEL
        mask = (idxv >= 0) & (idxv < n_past_blk) & in_half & (pos <= t_pos)
        s = jnp.where(mask, _dot(q4, kmat), -jnp.inf)
        c0 = NSA_ROW + g * NSA_DH
        new_k = newkv[:, c0:c0 + NSA_DH]
        new_v = newkv[:, c0 + NSA_G * NSA_DH:c0 + NSA_G * NSA_DH + NSA_DH]
        s_n = jnp.sum(q4f * new_k.astype(BF).astype(F32), -1, keepdims=True)
        s_n = jnp.where(jnp.logical_and(any_new, new_visible), s_n, -jnp.inf)
        o_slc = with_new_key(s, s_n, vmat, new_v)
        c1 = 2 * NSA_ROW + g * NSA_DH
        nwk = newkv[:, c1:c1 + NSA_DH]
        nwv = newkv[:, c1 + NSA_G * NSA_DH:c1 + NSA_G * NSA_DH + NSA_DH]
        s_w = jnp.where(m_w, _dot(q4, win_ref[0, 0, 0, g].astype(BF)), -jnp.inf)
        s_wn = jnp.sum(q4f * nwk.astype(BF).astype(F32), -1, keepdims=True)
        o_win = with_new_key(s_w, s_wn, win_ref[0, 0, 1, g].astype(BF), nwv)
        for hp in range(NSA_HPG):
            hcol = slice((g * NSA_HPG + hp) * NSA_DH, (g * NSA_HPG + hp + 1) * NSA_DH)
            outs.append(_head_gate(gates, g, hp, 0) * ocmp[:, hcol] + _head_gate(gates, g, hp, 1) * o_slc[hp:hp + 1]
                        + _head_gate(gates, g, hp, 2) * o_win[hp:hp + 1])
    o_ref[0] = jnp.concatenate(outs, axis=1).astype(BF)


def _sample_attn(idx_flat, pt_flat, slc_t, u_s3, win_t, ocmp, layer, db, n_pages, t_pos, l_buf):
    sub_pp = PAGE_SIZE // L_SEL
    n_past_blk = n_pages * sub_pp

    def blk_map(g, k):
        def f(b, idx, pt):
            pi = jnp.clip(idx[(b * NSA_G + g) * N_TOP + k], 0, n_past_blk - 1)
            return (layer, pt[b * n_pages + pi // sub_pp], 0, g, 0, 0)
        return f

    fix3 = lambda col: (lambda b, idx, pt: (b, 0, col))
    grid_spec = pltpu.PrefetchScalarGridSpec(
        num_scalar_prefetch=2,
        grid=(db,),
        in_specs=[pl.BlockSpec((1, 1, 2, 1, NSA_DH, PAGE_SIZE), blk_map(g, k))
                  for g in range(NSA_G) for k in range(N_TOP)]
        + [pl.BlockSpec((1, 1, NSA_W), fix3(P_NQ // NSA_W)),
           pl.BlockSpec((1, 1, 128), fix3(P_SMALL // 128)),
           pl.BlockSpec((1, 1, NSA_KV), fix3(P_NKV // NSA_KV)),
           pl.BlockSpec((1, 1, 2, NSA_G, NSA_DH, l_buf), lambda b, idx, pt: (layer, b, 0, 0, 0, 0)),
           pl.BlockSpec((1, 1, NSA_W), fix3(0))],
        out_specs=pl.BlockSpec((1, 1, NSA_W), fix3(0)),
    )
    return pl.pallas_call(
        functools.partial(_sample_attn_kernel, t_pos=t_pos, n_past_blk=n_past_blk, l_buf=l_buf),
        grid_spec=grid_spec,
        out_shape=jax.ShapeDtypeStruct((db, 1, NSA_W), BF),
        compiler_params=_params("arbitrary"),
        name="nsa_sample_attn",
    )(idx_flat, pt_flat, *([slc_t] * (NSA_G * N_TOP)), u_s3, u_s3, u_s3, win_t, ocmp)


def _pack_w_in(w):
    zeros = jnp.zeros((w.shape[0], 128 - 2 * DN_H - 3 * NSA_H + 128), w.dtype)
    return jnp.concatenate(
        [w[:, OFF_MERGE:N_IN], w[:, 0:DN_CONV_C], w[:, OFF_NSA_KV:OFF_NSA_G], w[:, OFF_DN_B:OFF_CF],
         w[:, OFF_NSA_G:OFF_MERGE], zeros, w[:, OFF_DN_Z:OFF_DN_B], w[:, OFF_CF:OFF_NSA_Q],
         w[:, OFF_NSA_Q:OFF_NSA_KV]], axis=1).astype(BF)


def _pack_cmp_weights(pe, w1, w2):
    eye = jnp.eye(2, dtype=F32)
    w1r = w1.reshape(2, L_CMP, NSA_DH, NSA_DH)
    w1big = jnp.einsum("slde,sp,gq->lsgdpqe", w1r, eye, eye).reshape(CMP_FLAT, NSA_ROW).astype(BF)
    w2big = jnp.einsum("sde,sp,gq->sgdpqe", w2, eye, eye).reshape(NSA_ROW, NSA_ROW).astype(BF)
    pe_flat = jnp.broadcast_to(pe.transpose(1, 0, 2)[:, :, None, :], (L_CMP, 2, NSA_G, NSA_DH)).reshape(1, CMP_FLAT)
    return pe_flat, w1big, w2big


def _pack_dn_par(a_log, dt_bias):
    par = jnp.zeros((8, 128), F32)
    par = par.at[0, SM_A:SM_A + DN_H].set(a_log)
    return par.at[1, SM_A:SM_A + DN_H].set(dt_bias)


def _pack_ffn(w_in, w_out):
    fc = D_FF // FFN_CHUNKS
    wi = jnp.stack([jnp.concatenate([w_in[:, c * fc:(c + 1) * fc], w_in[:, D_FF + c * fc:D_FF + (c + 1) * fc]], axis=1)
                    for c in range(FFN_CHUNKS)]).astype(BF)
    wo = w_out.reshape(FFN_CHUNKS, fc, D_MODEL).astype(BF)
    return wi, wo


def kernel(x_prompt, x_sample, cache_nsa_cmp, cache_nsa_slc, cache_nsa_win, state_dn, state_dn_conv, state_cf_conv, page_table, w_in, dn_conv_w, dn_a_log, dn_dt_bias, dn_norm_g, cf_dw_w, cf_dw_b, cf_ln_g, cf_ln_b, nsa_cmp_pe, nsa_cmp_w1, nsa_cmp_w2, w_branch_a, w_branch_b, w_branch_c, w_out, ln1_g, ln1_b, ln2_g, ln2_b, w_ffn_in, w_ffn_out):
    bsz, t, d = x_prompt.shape
    db = x_sample.shape[0]
    depth = w_in.shape[0]
    n_pool = cache_nsa_cmp.shape[1]
    n_pages = page_table.shape[1]
    past = n_pages * PAGE_SIZE
    l_buf = cache_nsa_win.shape[2]
    alpha = _alpha(depth)
    assert x_sample.shape[1] == 1 and d == D_MODEL
    assert t % CF_TILE == 0 and t >= WINDOW + Q_BLOCK and t % SEL_TK == 0 and n_pages % min(CMP_PAGES, n_pages) == 0

    cmp_t = jnp.transpose(cache_nsa_cmp, (0, 1, 3, 4, 5, 2)).reshape(depth, n_pool, NSA_ROW, PAGE_SIZE)
    slc_t = jnp.transpose(cache_nsa_slc, (0, 1, 3, 4, 5, 2))
    win_t = jnp.transpose(cache_nsa_win, (0, 1, 3, 4, 5, 2))
    pt_flat = page_table.reshape(-1)
    tm_p = 512 if (bsz * t) % 512 == 0 else 256
    wlen = min(WINDOW, t)

    xp = x_prompt.reshape(bsz * t, d)
    xs = x_sample.reshape(db, d)
    st = {k: [] for k in ("cmp_p", "cmp_s", "slc_p", "slc_s", "win_p", "win_s", "dn_p", "dn_s", "dnc_p", "dnc_s",
                          "cfc_p", "cfc_s")}
    for l in range(depth):
        wr = _pack_w_in(w_in[l])
        w_qkv_t = w_in[l][:, OFF_NSA_Q:OFF_NSA_G].T.astype(BF)
        pe_flat, w1big, w2big = _pack_cmp_weights(nsa_cmp_pe[l], nsa_cmp_w1[l], nsa_cmp_w2[l])
        par = _pack_dn_par(dn_a_log[l], dn_dt_bias[l])
        ng = dn_norm_g[l].reshape(1, DN_DV)
        fbias, fg, fb = cf_dw_b[l].reshape(1, CF_C), cf_ln_g[l].reshape(1, CF_C), cf_ln_b[l].reshape(1, CF_C)
        pa, pb, pc, wo = (w_branch_a[l].astype(BF), w_branch_b[l].astype(BF), w_branch_c[l].astype(BF),
                          w_out[l].astype(BF))
        wi_f, wo_f = _pack_ffn(w_ffn_in[l], w_ffn_out[l])
        g1, b1, g2, b2 = (ln1_g[l].reshape(1, d), ln1_b[l].reshape(1, d), ln2_g[l].reshape(1, d),
                          ln2_b[l].reshape(1, d))

        u_p = _matmul(xp, wr, tm_p, 1536)
        u_p3 = u_p.reshape(bsz, t, N_PAD)
        qkvt_bf, kvt = _proj_nt(xp.reshape(bsz, t, d), w_qkv_t, 512)
        oa, dn_state = _dn_prompt(u_p3, dn_conv_w[l], par, ng)
        ob, cf_tail = _cf_prompt(u_p, cf_dw_w[l], fbias, fg, fb, bsz, t)
        kvc, kvct = _compress_prompt(u_p[:, P_NKV:P_NKV + NSA_ROW].reshape(bsz, t // L_CMP, CMP_FLAT), pe_flat, w1big,
                                     w2big)
        kv_bf = u_p3[:, :, P_NKV:P_NKV + NSA_KV].astype(BF)
        oc = _nsa_prompt(u_p, qkvt_bf, kvc, kvct, kv_bf, bsz, t)
        x1 = _merge(xp, oa.reshape(bsz * t, DN_W), ob, oc, u_p, pa, pb, pc, wo, g1, b1, 256, alpha)
        xp_next = _ffn(x1, wi_f, wo_f, g2, b2, tm_p, alpha)
        kvt6 = kvt.reshape(bsz, 3, 2, NSA_G, NSA_DH, t)
        st["cmp_p"].append(jnp.transpose(kvt6[:, 0], (0, 4, 1, 2, 3)))
        st["slc_p"].append(jnp.transpose(kvt6[:, 1], (0, 4, 1, 2, 3)))
        st["win_p"].append(jnp.transpose(kvt6[:, 2, :, :, :, t - wlen:], (0, 4, 1, 2, 3)))
        st["dn_p"].append(dn_state)
        st["dnc_p"].append(u_p3[:, t - (DN_CONV - 1):, P_DNQKV:P_DNQKV + DN_CONV_C])
        st["cfc_p"].append(cf_tail[:, CF_HALO - (CF_K - 1):])

        u_s = _matmul(xs, wr, db, 1536)
        u_s3 = u_s.reshape(db, 1, N_PAD)
        oa_s, ob_s, dn_state_s, glu_s = _sample_mix(u_s3, state_dn, state_dn_conv, state_cf_conv, l, dn_conv_w[l],
                                                    par, ng, cf_dw_w[l], fbias, fg, fb, db)
        pe_t = jnp.broadcast_to(jnp.tile(nsa_cmp_pe[l].transpose(0, 2, 1), (1, 1, BLK_PER_PAGE))[:, None],
                                (2, NSA_G, NSA_DH, PAGE_SIZE)).reshape(NSA_ROW, PAGE_SIZE)
        kvc_s = _compress_sample(cmp_t, pt_flat, l, pe_t, w1big.reshape(L_CMP, NSA_ROW, NSA_ROW), w2big, db,
                                 n_pages)
        n_sel = n_pages * (PAGE_SIZE // L_SEL) + 1
        ocmp_s, idx = _sample_select(u_s3, kvc_s, db, past, n_sel)
        oc_s = _sample_attn(idx[:, :, :N_TOP].reshape(-1), pt_flat, slc_t, u_s3, win_t, ocmp_s, l, db, n_pages,
                            past, l_buf)
        x1s = _merge(xs, oa_s.reshape(db, DN_W), ob_s.reshape(db, CF_C), oc_s.reshape(db, NSA_W), u_s, pa, pb, pc,
                     wo, g1, b1, db, alpha)
        xs_next = _ffn(x1s, wi_f, wo_f, g2, b2, db, alpha)
        kv_s = u_s[:, P_NKV:P_NKV + NSA_KV].reshape(db, 1, 3, 2, NSA_G, NSA_DH)
        st["cmp_s"].append(kv_s[:, :, 0])
        st["slc_s"].append(kv_s[:, :, 1])
        st["win_s"].append(jnp.concatenate([cache_nsa_win[l], kv_s[:, :, 2]], axis=1)[:, -min(WINDOW, l_buf + 1):])
        st["dn_s"].append(dn_state_s)
        st["dnc_s"].append(jnp.concatenate([state_dn_conv[l], u_s[:, None, P_DNQKV:P_DNQKV + DN_CONV_C]],
                                           axis=1)[:, -(DN_CONV - 1):])
        st["cfc_s"].append(jnp.concatenate([state_cf_conv[l], glu_s], axis=1)[:, -(CF_K - 1):])
        xp, xs = xp_next, xs_next

    stk = {k: jnp.stack(v) for k, v in st.items()}
    return (xp.reshape(bsz, t, d), xs.reshape(db, 1, d), stk["cmp_p"], stk["cmp_s"], stk["slc_p"], stk["slc_s"],
            stk["win_p"], stk["win_s"], stk["dn_p"], stk["dn_s"], stk["dnc_p"], stk["dnc_s"], stk["cfc_p"],
            stk["cfc_s"])
```

```python
import functools

import jax
import jax.numpy as jnp
from jax import lax
from jax.experimental import pallas as pl
from jax.experimental.pallas import tpu as pltpu

F32 = jnp.float32
BF = jnp.bfloat16
HI = lax.Precision.HIGHEST

D_MODEL = 1024
DN_H = 4
DN_DK = 128
DN_DV = 128
DN_QK = DN_H * DN_DK
DN_W = DN_H * DN_DV
DN_CONV_C = 2 * DN_QK + DN_W
DN_CONV = 4
DN_CHUNK = 64
CF_C = 512
CF_K = 31
NSA_H = 8
NSA_G = 2
NSA_HPG = NSA_H // NSA_G
NSA_DH = 64
NSA_W = NSA_H * NSA_DH
NSA_KV = 3 * 2 * NSA_G * NSA_DH
NSA_ROW = 2 * NSA_G * NSA_DH
L_CMP = 32
L_SEL = 64
N_TOP = 16
WINDOW = 512
Q_BLOCK = 128
FORCE_SCORE = 1.0e4
PAGE_SIZE = 128
D_FF = 2816
FFN_CHUNKS = 2
NEG_BIG = -1.0e30

OFF_DN_Z = DN_CONV_C
OFF_DN_B = OFF_DN_Z + DN_W
OFF_DN_A = OFF_DN_B + DN_H
OFF_CF = OFF_DN_A + DN_H
OFF_NSA_Q = OFF_CF + 2 * CF_C
OFF_NSA_KV = OFF_NSA_Q + NSA_W
OFF_NSA_G = OFF_NSA_KV + NSA_KV
OFF_MERGE = OFF_NSA_G + 3 * NSA_H
N_IN = OFF_MERGE + 3 * D_MODEL

P_MERGE = 0
P_DNQKV = 3072
P_NKV = 4608
P_SMALL = 5376
P_DNZ = 5632
P_CF = 6144
P_NQ = 7168
N_PAD = 7680
SM_B = 0
SM_A = DN_H
SM_G = 2 * DN_H

def _alpha(depth):
    return (2 * depth) ** 0.25


def _dot(a, b, precision=None):
    return jnp.dot(a, b, preferred_element_type=F32, precision=precision)


def _dot_nt(a, b):
    return lax.dot_general(a, b, (((1,), (1,)), ((), ())), preferred_element_type=F32)


def _dot_tn(a, b):
    return lax.dot_general(a, b, (((0,), (0,)), ((), ())), preferred_element_type=F32)


def _silu(x):
    return x * jax.nn.sigmoid(x)


def _softplus(x):
    return jnp.maximum(x, 0.0) + jnp.log(1.0 + jnp.exp(-jnp.abs(x)))


def _l2norm(x):
    return x * lax.rsqrt(jnp.sum(x * x, -1, keepdims=True) + 1e-6)


def _layer_norm(x, g, b):
    xc = x - jnp.mean(x, -1, keepdims=True)
    var = jnp.mean(xc * xc, -1, keepdims=True)
    return xc * lax.rsqrt(var + 1e-5) * g + b


def _masked_softmax(s, mask):
    s = jnp.where(mask, s, -jnp.inf)
    m = jnp.max(s, -1, keepdims=True)
    m = jnp.where(m > -jnp.inf, m, 0.0)
    e = jnp.exp(s - m)
    den = jnp.sum(e, -1, keepdims=True)
    return e / jnp.where(den > 0, den, 1.0)


def _params(*sem, vmem_mb=None):
    kw = {}
    if vmem_mb is not None:
        kw["vmem_limit_bytes"] = vmem_mb * 1024 * 1024
    return pltpu.CompilerParams(dimension_semantics=sem, **kw)


def _mm_kernel(x_ref, w_ref, o_ref):
    o_ref[...] = _dot(x_ref[...].astype(BF), w_ref[...])


def _matmul(x, w, tm, tn):
    m, k = x.shape
    n = w.shape[1]
    return pl.pallas_call(
        _mm_kernel,
        grid=(m // tm, n // tn),
        in_specs=[pl.BlockSpec((tm, k), lambda i, j: (i, 0)),
                  pl.BlockSpec((k, tn), lambda i, j: (0, j))],
        out_specs=pl.BlockSpec((tm, tn), lambda i, j: (i, j)),
        out_shape=jax.ShapeDtypeStruct((m, n), F32),
        compiler_params=_params("arbitrary", "arbitrary", vmem_mb=48),
        name="proj_matmul",
    )(x, w)


def _dn_gates(sm, par):
    beta = jax.nn.sigmoid(sm)
    g = -jnp.exp(par[0:1, :]) * _softplus(sm + par[1:2, :])
    return beta, g


def _split2(x):
    hi = x.astype(BF)
    return hi, (x - hi.astype(F32)).astype(BF)


def _dot_split(a, b):
    return _dot(a[0], b[0]) + _dot(a[0], b[1]) + _dot(a[1], b[0])


def _dn_prep_kernel(qkv_ref, sm_ref, cw_ref, par_ref, w_ref, kc_ref, qd_ref, kd_ref, qk_ref, gl_ref, xp_ref):
    n = pl.program_id(0)
    nb = qkv_ref.shape[0]
    c = DN_CHUNK

    @pl.when(n == 0)
    def _():
        xp_ref[:, 0:8, :] = jnp.zeros((nb, 8, DN_CONV_C), F32)

    cw = cw_ref[...]
    par = par_ref[...]
    ii = lax.broadcasted_iota(jnp.int32, (c, c), 0)
    jj = lax.broadcasted_iota(jnp.int32, (c, c), 1)
    tri = jnp.where(ii >= jj, 1.0, 0.0).astype(BF)
    eye = jnp.where(ii == jj, 1.0, 0.0).astype(F32)
    ch = []
    for bi in range(nb):
        raw = qkv_ref[bi]
        xp_ref[bi, 8:8 + c, :] = raw
        conv = cw[0:1, :] * xp_ref[bi, 5:5 + c, :]
        for k in range(1, DN_CONV):
            conv = conv + cw[k:k + 1, :] * xp_ref[bi, 5 + k:5 + k + c, :]
        xp_ref[bi, 0:8, :] = raw[c - 8:c, :]
        h = _silu(conv)
        beta_all, g_all = _dn_gates(sm_ref[bi], par)
        g1 = g_all.astype(BF)
        r1 = g_all - g1.astype(F32)
        g2 = r1.astype(BF)
        g3 = (r1 - g2.astype(F32)).astype(BF)
        gcum = _dot(tri, g1) + _dot(tri, g2) + _dot(tri, g3)
        gcum_t = gcum.T
        gl_ref[0, bi:bi + 1, :] = jnp.exp(gcum[c - 1:c, :])
        for hd in range(DN_H):
            qh = _l2norm(h[:, hd * DN_DK:(hd + 1) * DN_DK]) * (DN_DK ** -0.5)
            kh = _l2norm(h[:, DN_QK + hd * DN_DK:DN_QK + (hd + 1) * DN_DK])
            vh = h[:, 2 * DN_QK + hd * DN_DV:2 * DN_QK + (hd + 1) * DN_DV]
            bcol = beta_all[:, SM_B + hd:SM_B + hd + 1]
            gc = gcum[:, SM_A + hd:SM_A + hd + 1]
            gr = gcum_t[SM_A + hd:SM_A + hd + 1, :]
            dec_incl = jnp.exp(jnp.where(ii >= jj, gc - gr, -jnp.inf))
            ch.append(dict(bi=bi, hd=hd, qh=qh, kh=kh, vh=vh, bcol=bcol, gc=gc, dec_incl=dec_incl))
    for s in ch:
        kb = s["kh"].astype(BF)
        both = _dot_nt(jnp.concatenate([s["qh"].astype(BF), kb], axis=0), kb)
        s["qk"] = both[:c] * s["dec_incl"]
        nm = -(s["bcol"] * both[c:] * jnp.where(ii > jj, s["dec_incl"], 0.0))
        s["inv"] = eye + nm
        s["pw"] = _split2(nm)
    for _ in range(5):
        for s in ch:
            s["pw"] = _split2(_dot_split(s["pw"], s["pw"]))
        for s in ch:
            s["inv"] = s["inv"] + _dot_split(_split2(s["inv"]), s["pw"])
    for s in ch:
        eg = jnp.exp(s["gc"])
        rhs = jnp.concatenate([s["bcol"] * s["vh"], (s["bcol"] * eg) * s["kh"]], axis=1)
        xsol = _dot_split(_split2(s["inv"]), _split2(rhs))
        bi = s["bi"]
        col = slice(s["hd"] * DN_DV, (s["hd"] + 1) * DN_DV)
        w_ref[bi, :, col] = xsol[:, :DN_DV]
        kc_ref[bi, :, col] = xsol[:, DN_DV:].astype(BF)
        qd_ref[bi, :, col] = (s["qh"] * eg).astype(BF)
        kd_ref[bi, :, col] = (s["kh"] * jnp.exp(s["gc"][c - 1:c, :] - s["gc"])).astype(BF)
        qk_ref[bi, :, col] = jnp.concatenate([s["qk"], jnp.zeros((c, DN_DV - c), F32)], axis=1).astype(BF)


def _dn_scan_kernel(w_ref, kc_ref, qd_ref, kd_ref, qk_ref, gl_ref, z_ref, ng_ref, oa_ref, st_ref, s_ref, *, n_chunks):
    n = pl.program_id(0)
    c = DN_CHUNK

    @pl.when(n == 0)
    def _():
        s_ref[...] = jnp.zeros_like(s_ref)

    ng = ng_ref[...]
    for bi in range(w_ref.shape[0]):
        g_last = gl_ref[0, bi:bi + 1, :]
        z = z_ref[bi]
        outs = []
        for hd in range(DN_H):
            col = slice(hd * DN_DV, (hd + 1) * DN_DV)
            s_old = s_ref[bi, hd]
            sb = s_old.astype(BF)
            u = w_ref[bi, :, col] - _dot(kc_ref[bi, :, col], sb)
            ub = u.astype(BF)
            o = _dot(qd_ref[bi, :, col], sb) + _dot(qk_ref[bi, :, hd * DN_DV:hd * DN_DV + c], ub)
            s_ref[bi, hd] = g_last[:, SM_A + hd:SM_A + hd + 1] * s_old + _dot_tn(kd_ref[bi, :, col], ub)
            o = o * lax.rsqrt(jnp.mean(o * o, -1, keepdims=True) + 1e-6) * ng
            outs.append(o * _silu(z[:, col]))
        oa_ref[bi] = jnp.concatenate(outs, axis=1).astype(BF)

    @pl.when(n == n_chunks - 1)
    def _():
        st_ref[...] = s_ref[...]


def _dn_prompt(u_p3, conv_w, par, ng):
    b, t, _ = u_p3.shape
    nc = t // DN_CHUNK
    c = DN_CHUNK
    row = lambda n: (0, n, 0)
    fix = lambda n: (0, 0)
    w_val, k_cum, q_dec, k_dec, qk, g_last = pl.pallas_call(
        _dn_prep_kernel,
        grid=(nc,),
        in_specs=[pl.BlockSpec((b, c, DN_CONV_C), lambda n: (0, n, P_DNQKV // DN_CONV_C)),
                  pl.BlockSpec((b, c, 128), lambda n: (0, n, P_SMALL // 128)),
                  pl.BlockSpec((DN_CONV, DN_CONV_C), fix),
                  pl.BlockSpec((8, 128), fix)],
        out_specs=[pl.BlockSpec((b, c, DN_W), row)] * 5 + [pl.BlockSpec((1, b, 128), lambda n: (n, 0, 0))],
        out_shape=[jax.ShapeDtypeStruct((b, t, DN_W), F32)] + [jax.ShapeDtypeStruct((b, t, DN_W), BF)] * 4
        + [jax.ShapeDtypeStruct((nc, b, 128), F32)],
        scratch_shapes=[pltpu.VMEM((b, 8 + c, DN_CONV_C), F32)],
        compiler_params=_params("arbitrary", vmem_mb=48),
        name="dn_prep",
    )(u_p3, u_p3, conv_w, par)
    return pl.pallas_call(
        functools.partial(_dn_scan_kernel, n_chunks=nc),
        grid=(nc,),
        in_specs=[pl.BlockSpec((b, c, DN_W), row)] * 5
        + [pl.BlockSpec((1, b, 128), lambda n: (n, 0, 0)),
           pl.BlockSpec((b, c, DN_W), lambda n: (0, n, P_DNZ // DN_W)),
           pl.BlockSpec((1, DN_DV), fix)],
        out_specs=[pl.BlockSpec((b, c, DN_W), row),
                   pl.BlockSpec((b, DN_H, DN_DK, DN_DV), lambda n: (0, 0, 0, 0))],
        out_shape=[jax.ShapeDtypeStruct((b, t, DN_W), BF),
                   jax.ShapeDtypeStruct((b, DN_H, DN_DK, DN_DV), F32)],
        scratch_shapes=[pltpu.VMEM((b, DN_H, DN_DK, DN_DV), F32)],
        compiler_params=_params("arbitrary"),
        name="dn_scan",
    )(w_val, k_cum, q_dec, k_dec, qk, g_last, u_p3, ng)


CF_TILE = 256
CF_ROWS = 32
CF_HALO = 32


def _cf_prompt_kernel(u_ref, w_ref, bias_ref, g_ref, b_ref, ob_ref, tail_ref, buf_ref, *, n_tiles):
    i = pl.program_id(1)
    tc = CF_TILE

    @pl.when(i == 0)
    def _():
        buf_ref[0:CF_HALO, :] = jnp.zeros((CF_HALO, CF_C), F32)

    u = u_ref[...]
    buf_ref[CF_HALO:CF_HALO + tc, :] = u[:, :CF_C] * jax.nn.sigmoid(u[:, CF_C:])
    bias = bias_ref[...]
    g = g_ref[...]
    b = b_ref[...]
    off = CF_HALO - (CF_K - 1)
    for r0 in range(0, tc, CF_ROWS):
        acc = bias + w_ref[0:1, :] * buf_ref[r0 + off:r0 + off + CF_ROWS, :]
        for k in range(1, CF_K):
            acc = acc + w_ref[k:k + 1, :] * buf_ref[r0 + off + k:r0 + off + k + CF_ROWS, :]
        ob_ref[r0:r0 + CF_ROWS, :] = _silu(_layer_norm(acc, g, b)).astype(BF)
    tail = buf_ref[tc:tc + CF_HALO, :]
    buf_ref[0:CF_HALO, :] = tail

    @pl.when(i == n_tiles - 1)
    def _():
        tail_ref[0] = tail


def _cf_prompt(u_p, w, bias, g, bb, b, t):
    nt = t // CF_TILE
    return pl.pallas_call(
        functools.partial(_cf_prompt_kernel, n_tiles=nt),
        grid=(b, nt),
        in_specs=[pl.BlockSpec((CF_TILE, 2 * CF_C), lambda i, n: (i * nt + n, P_CF // (2 * CF_C))),
                  pl.BlockSpec((CF_K, CF_C), lambda i, n: (0, 0)),
                  pl.BlockSpec((1, CF_C), lambda i, n: (0, 0)),
                  pl.BlockSpec((1, CF_C), lambda i, n: (0, 0)),
                  pl.BlockSpec((1, CF_C), lambda i, n: (0, 0))],
        out_specs=[pl.BlockSpec((CF_TILE, CF_C), lambda i, n: (i * nt + n, 0)),
                   pl.BlockSpec((1, CF_HALO, CF_C), lambda i, n: (i, 0, 0))],
        out_shape=[jax.ShapeDtypeStruct((b * t, CF_C), BF),
                   jax.ShapeDtypeStruct((b, CF_HALO, CF_C), F32)],
        scratch_shapes=[pltpu.VMEM((CF_HALO + CF_TILE, CF_C), F32)],
        compiler_params=_params("arbitrary", "arbitrary"),
        name="cf_prompt",
    )(u_p, w, bias, g, bb)


CMP_FLAT = L_CMP * NSA_ROW


def _cmp_mlp(x, pe, w1, w2):
    xb = (x + pe).astype(BF)
    h = _silu(_dot(xb, w1))
    return _dot(h.astype(BF), w2)


def _cmp_kernel(x_ref, pe_ref, w1_ref, w2_ref, o_ref, ot_ref):
    kvc = _cmp_mlp(x_ref[0], pe_ref[...], w1_ref[...], w2_ref[...])
    o_ref[0] = kvc
    ot_ref[0] = kvc.T


def _compress_prompt(x, pe, w1, w2):
    b, nc, _ = x.shape
    return pl.pallas_call(
        _cmp_kernel,
        grid=(b,),
        in_specs=[pl.BlockSpec((1, nc, CMP_FLAT), lambda i: (i, 0, 0)),
                  pl.BlockSpec((1, CMP_FLAT), lambda i: (0, 0)),
                  pl.BlockSpec((CMP_FLAT, NSA_ROW), lambda i: (0, 0)),
                  pl.BlockSpec((NSA_ROW, NSA_ROW), lambda i: (0, 0))],
        out_specs=[pl.BlockSpec((1, nc, NSA_ROW), lambda i: (i, 0, 0)),
                   pl.BlockSpec((1, NSA_ROW, nc), lambda i: (i, 0, 0))],
        out_shape=[jax.ShapeDtypeStruct((b, nc, NSA_ROW), F32),
                   jax.ShapeDtypeStruct((b, NSA_ROW, nc), F32)],
        compiler_params=_params("arbitrary"),
        name="nsa_compress_prompt",
    )(x, pe, w1, w2)


NT_ROWS = NSA_W + NSA_KV


def _proj_nt_kernel(x_ref, w_ref, obf_ref, okv_ref):
    r = _dot_nt(w_ref[...], x_ref[0].astype(BF))
    obf_ref[0] = r.astype(BF)
    okv_ref[0] = r[NSA_W:, :]


def _proj_nt(x3, w_t, tm):
    b, t, d = x3.shape
    return pl.pallas_call(
        _proj_nt_kernel,
        grid=(b, t // tm),
        in_specs=[pl.BlockSpec((1, tm, d), lambda i, j: (i, j, 0)),
                  pl.BlockSpec((NT_ROWS, d), lambda i, j: (0, 0))],
        out_specs=[pl.BlockSpec((1, NT_ROWS, tm), lambda i, j: (i, 0, j)),
                   pl.BlockSpec((1, NSA_KV, tm), lambda i, j: (i, 0, j))],
        out_shape=[jax.ShapeDtypeStruct((b, NT_ROWS, t), BF),
                   jax.ShapeDtypeStruct((b, NSA_KV, t), F32)],
        compiler_params=_params("arbitrary", "arbitrary"),
        name="proj_qkv_t",
    )(x3, w_t)


CMP_PAGES = 64
BLK_PER_PAGE = PAGE_SIZE // L_CMP


def _cmp_sample_kernel(pt_ref, cmp_hbm, pet_ref, w1_ref, w2_ref, o_ref, buf_ref, xs_ref, sem, *, layer, n_steps,
                       n_pg):
    s = pl.program_id(0)
    slot = lax.rem(s, 2)

    def page_copy(step, j, into):
        return pltpu.make_async_copy(cmp_hbm.at[layer, pt_ref[step * n_pg + j]], buf_ref.at[into, j], sem.at[into])

    @pl.when(s == 0)
    def _():
        for j in range(n_pg):
            page_copy(0, j, 0).start()

    @pl.when(s + 1 < n_steps)
    def _():
        for j in range(n_pg):
            page_copy(s + 1, j, 1 - slot).start()

    for j in range(n_pg):
        page_copy(s, j, slot).wait()

    pet = pet_ref[...]
    half = NSA_ROW // 2
    pair_rows = 2 * PAGE_SIZE
    grp = 2 * BLK_PER_PAGE
    r_out = lax.broadcasted_iota(jnp.int32, (pair_rows, pair_rows), 0)
    r_in = lax.broadcasted_iota(jnp.int32, (pair_rows, pair_rows), 1)
    src = ((r_out // BLK_PER_PAGE) % 2) * PAGE_SIZE + (r_out % BLK_PER_PAGE) * L_CMP + r_out // grp
    perm = jnp.where(r_in == src, 1.0, 0.0).astype(BF)

    def to_rows(jp, carry):
        x2 = jnp.concatenate([buf_ref[slot, 2 * jp] + pet, buf_ref[slot, 2 * jp + 1] + pet], axis=1).astype(BF)
        xt = _dot_nt(perm, x2)
        r0 = pl.multiple_of(jp * grp, grp)
        for l in range(L_CMP):
            xs_ref[0, l, pl.ds(r0, grp), :] = xt[l * grp:(l + 1) * grp, :half]
            xs_ref[1, l, pl.ds(r0, grp), :] = xt[l * grp:(l + 1) * grp, half:]
        return carry

    lax.fori_loop(0, n_pg // 2, to_rows, 0, unroll=2)
    nblk = n_pg * BLK_PER_PAGE
    acc = [jnp.zeros((nblk, half), F32), jnp.zeros((nblk, half), F32)]
    for l in range(L_CMP):
        for hf in range(2):
            a = xs_ref[hf, l].astype(BF)
            acc[hf] = acc[hf] + _dot(a, w1_ref[l, hf * half:(hf + 1) * half, hf * half:(hf + 1) * half])
    o_ref[0] = _dot(_silu(jnp.concatenate(acc, axis=1)).astype(BF), w2_ref[...])


def _compress_sample(cmp_t, pt_flat, layer, pe_t, w1r, w2, db, n_pages):
    n_pg = min(CMP_PAGES, n_pages)
    n_steps = db * n_pages // n_pg
    nblk = n_pg * BLK_PER_PAGE
    grid_spec = pltpu.PrefetchScalarGridSpec(
        num_scalar_prefetch=1,
        grid=(n_steps,),
        in_specs=[pl.BlockSpec(memory_space=pl.ANY),
                  pl.BlockSpec((NSA_ROW, PAGE_SIZE), lambda s, pt: (0, 0)),
                  pl.BlockSpec((L_CMP, NSA_ROW, NSA_ROW), lambda s, pt: (0, 0, 0)),
                  pl.BlockSpec((NSA_ROW, NSA_ROW), lambda s, pt: (0, 0))],
        out_specs=pl.BlockSpec((1, nblk, NSA_ROW), lambda s, pt: (s, 0, 0)),
        scratch_shapes=[pltpu.VMEM((2, n_pg, NSA_ROW, PAGE_SIZE), F32),
                        pltpu.VMEM((2, L_CMP, nblk, NSA_ROW // 2), F32),
                        pltpu.SemaphoreType.DMA((2,))],
    )
    out = pl.pallas_call(
        functools.partial(_cmp_sample_kernel, layer=layer, n_steps=n_steps, n_pg=n_pg),
        grid_spec=grid_spec,
        out_shape=jax.ShapeDtypeStruct((n_steps, nblk, NSA_ROW), F32),
        compiler_params=_params("arbitrary", vmem_mb=48),
        name="nsa_compress_sample",
    )(pt_flat, cmp_t, pe_t, w1r, w2)
    return out.reshape(db, n_pages * BLK_PER_PAGE, NSA_ROW)


SEL_TK = 512


def _pair_matrix(nc, n_sel_pad):
    ci = lax.broadcasted_iota(jnp.int32, (nc, n_sel_pad), 0)
    si = lax.broadcasted_iota(jnp.int32, (nc, n_sel_pad), 1)
    return jnp.where((ci >> 1) == si, 1.0, 0.0).astype(F32)


def _head_gate(gates, g, hp, br):
    col = SM_G + (g * NSA_HPG + hp) * 3 + br
    return gates[:, col:col + 1]


def _masked_softmax0(s, mask):
    s = jnp.where(mask, s, -jnp.inf)
    m = jnp.max(s, 0, keepdims=True)
    m = jnp.where(m > -jnp.inf, m, 0.0)
    e = jnp.exp(s - m)
    den = jnp.sum(e, 0, keepdims=True)
    return e * (1.0 / jnp.where(den > 0, den, 1.0))


def _lane_tile(x, n):
    return jnp.concatenate([x] * n, axis=1)


def _nsa_prompt_kernel(qt_ref, sm_ref, kvc_ref, kvct_ref, ks_ref, kw_ref, vst_ref, vwt_ref, o_ref, sel_ref, *, t_len):
    i = pl.program_id(1)
    qb = Q_BLOCK
    cols = NSA_HPG * qb
    start = i * qb
    nc = t_len // L_CMP
    n_sel = t_len // L_SEL
    blk_per_tile = SEL_TK // L_SEL
    qt = qt_ref[0] * (NSA_DH ** -0.5)
    gates_t = jax.nn.sigmoid(sm_ref[...]).T
    t1 = start + lax.broadcasted_iota(jnp.int32, (1, qb), 1)
    t4 = start + (lax.broadcasted_iota(jnp.int32, (1, cols), 1) & (qb - 1))
    q4 = [jnp.concatenate([qt[(g * NSA_HPG + hp) * NSA_DH:(g * NSA_HPG + hp + 1) * NSA_DH, :]
                           for hp in range(NSA_HPG)], axis=1) for g in range(NSA_G)]
    zq = jnp.zeros((NSA_DH, cols), BF)
    qbd = jnp.concatenate([jnp.concatenate([q4[0], zq], axis=1), jnp.concatenate([zq, q4[1]], axis=1)], axis=0)

    kvc = kvc_ref[0]
    kvct = kvct_ref[0]
    s_c = _dot(kvc[:, 0:NSA_G * NSA_DH].astype(BF), qbd)
    c_end = (lax.broadcasted_iota(jnp.int32, (nc, 1), 0) + 1) * L_CMP - 1
    m_c = c_end <= t4
    pair_t = jnp.where((lax.broadcasted_iota(jnp.int32, (n_sel, nc), 1) >> 1)
                       == lax.broadcasted_iota(jnp.int32, (n_sel, nc), 0), 1.0, 0.0).astype(F32)
    blk2 = lax.broadcasted_iota(jnp.int32, (n_sel, qb), 0)
    cur = t1 >> 6
    valid = blk2 * L_SEL <= t1
    forced = (blk2 == 0) | (blk2 == cur) | (blk2 == cur - 1)
    o_cmp = []
    for g in range(NSA_G):
        p_c = _masked_softmax0(s_c[:, g * cols:(g + 1) * cols], m_c)
        vct = kvct[NSA_G * NSA_DH + g * NSA_DH:NSA_G * NSA_DH + (g + 1) * NSA_DH, :].astype(BF)
        o_cmp.append(_dot(vct, p_c.astype(BF)))
        imp_c = p_c[:, 0:qb] + p_c[:, qb:2 * qb] + p_c[:, 2 * qb:3 * qb] + p_c[:, 3 * qb:4 * qb]
        imp = _dot(pair_t, imp_c, precision=HI)
        score = jnp.where(valid, jnp.where(forced, FORCE_SCORE, imp), -jnp.inf)
        chunks = [score[8 * ci:8 * ci + 8, :] for ci in range(n_sel // 8)]
        ranks = [jnp.zeros((8, qb), F32) for _ in chunks]
        for r in range(n_sel):
            sr = score[r:r + 1, :]
            for ci, ch in enumerate(chunks):
                ge = jnp.where(sr >= ch, 1.0, 0.0)
                gt = jnp.where(sr > ch, 1.0, 0.0)
                if 8 * ci > r:
                    inc = ge
                elif 8 * ci + 7 <= r:
                    inc = gt
                else:
                    inc = jnp.where(blk2[8 * ci:8 * ci + 8, :] > r, ge, gt)
                ranks[ci] = ranks[ci] + inc
        rank = jnp.concatenate(ranks, axis=0)
        sel_ref[g] = jnp.where(rank < N_TOP, jnp.where(score > -jnp.inf, 1.0, 0.0), 0.0)

    n_kt = (start + qb + SEL_TK - 1) // SEL_TK

    def body(j, carry):
        k0 = pl.multiple_of(j * SEL_TK, SEL_TK)
        b0 = pl.multiple_of(j * blk_per_tile, blk_per_tile)
        s_both = _dot(ks_ref[0, pl.ds(k0, SEL_TK), :], qbd)
        causal = (k0 + lax.broadcasted_iota(jnp.int32, (SEL_TK, 1), 0)) <= t1
        out = []
        for g in range(NSA_G):
            m_i, l_i, acc = carry[3 * g:3 * g + 3]
            st = sel_ref[g, pl.ds(b0, blk_per_tile), :]
            mexp = jnp.concatenate([jnp.broadcast_to(st[bb:bb + 1, :], (L_SEL, qb)) for bb in range(blk_per_tile)],
                                   axis=0)
            bias = (jnp.where(causal, mexp, 0.0) - 1.0) * (-NEG_BIG)
            s = s_both[:, g * cols:(g + 1) * cols] + _lane_tile(bias, NSA_HPG)
            m_new = jnp.maximum(m_i, jnp.max(s, 0, keepdims=True))
            alpha = jnp.exp(m_i - m_new)
            p = jnp.exp(s - m_new)
            l_new = alpha * l_i + jnp.sum(p, 0, keepdims=True)
            vt = vst_ref[0, g * NSA_DH:(g + 1) * NSA_DH, pl.ds(k0, SEL_TK)]
            out += [m_new, l_new, alpha * acc + _dot(vt, p.astype(BF))]
        return tuple(out)

    init = (jnp.full((1, cols), NEG_BIG, F32), jnp.zeros((1, cols), F32), jnp.zeros((NSA_DH, cols), F32)) * NSA_G
    res = lax.fori_loop(0, n_kt, body, init)
    o_slc = [res[3 * g + 2] / jnp.where(res[3 * g + 1] > 0, res[3 * g + 1], 1.0) for g in range(NSA_G)]

    kstart = pl.multiple_of(jnp.maximum(start - WINDOW, 0), qb)
    wlen = WINDOW + qb
    s_w = _dot(kw_ref[0, pl.ds(kstart, wlen), :], qbd)
    dist = t1 - (kstart + lax.broadcasted_iota(jnp.int32, (wlen, 1), 0))
    bias_w = _lane_tile(jnp.where((dist >= 0) & (dist <= WINDOW), 0.0, NEG_BIG), NSA_HPG)
    outs = []
    for g in range(NSA_G):
        sw = s_w[:, g * cols:(g + 1) * cols] + bias_w
        e_w = jnp.exp(sw - jnp.max(sw, 0, keepdims=True))
        o_win = (_dot(vwt_ref[0, g * NSA_DH:(g + 1) * NSA_DH, pl.ds(kstart, wlen)], e_w.astype(BF))
                 * (1.0 / jnp.sum(e_w, 0, keepdims=True)))
        for hp in range(NSA_HPG):
            c = slice(hp * qb, (hp + 1) * qb)
            row = SM_G + (g * NSA_HPG + hp) * 3
            outs.append(gates_t[row:row + 1, :] * o_cmp[g][:, c] + gates_t[row + 1:row + 2, :] * o_slc[g][:, c]
                        + gates_t[row + 2:row + 3, :] * o_win[:, c])
    o_ref[...] = jnp.concatenate(outs, axis=0).T.astype(BF)


def _nsa_prompt(u_p, qkvt_bf, kvc, kvct, kv_bf, b, t):
    nq = t // Q_BLOCK
    nc = t // L_CMP
    k_blk = NSA_G * NSA_DH
    slc_k = NSA_ROW // k_blk
    win_k = 2 * NSA_ROW // k_blk
    slc_v = (NSA_W + NSA_ROW + k_blk) // k_blk
    win_v = (NSA_W + 2 * NSA_ROW + k_blk) // k_blk
    return pl.pallas_call(
        functools.partial(_nsa_prompt_kernel, t_len=t),
        grid=(b, nq),
        in_specs=[pl.BlockSpec((1, NSA_W, Q_BLOCK), lambda i, n: (i, 0, n)),
                  pl.BlockSpec((Q_BLOCK, 128), lambda i, n: (i * nq + n, P_SMALL // 128)),
                  pl.BlockSpec((1, nc, NSA_ROW), lambda i, n: (i, 0, 0)),
                  pl.BlockSpec((1, NSA_ROW, nc), lambda i, n: (i, 0, 0)),
                  pl.BlockSpec((1, t, k_blk), lambda i, n: (i, 0, slc_k)),
                  pl.BlockSpec((1, t, k_blk), lambda i, n: (i, 0, win_k)),
                  pl.BlockSpec((1, k_blk, t), lambda i, n: (i, slc_v, 0)),
                  pl.BlockSpec((1, k_blk, t), lambda i, n: (i, win_v, 0))],
        out_specs=pl.BlockSpec((Q_BLOCK, NSA_W), lambda i, n: (i * nq + n, 0)),
        out_shape=jax.ShapeDtypeStruct((b * t, NSA_W), BF),
        scratch_shapes=[pltpu.VMEM((NSA_G, t // L_SEL, Q_BLOCK), F32)],
        compiler_params=_params("arbitrary", "arbitrary", vmem_mb=48),
        name="nsa_prompt",
    )(qkvt_bf, u_p, kvc, kvct, kv_bf, kv_bf, qkvt_bf, qkvt_bf)


def _merge_kernel(x_ref, oa_ref, ob_ref, oc_ref, gm_ref, pa_ref, pb_ref, pc_ref, wo_ref, g_ref, b_ref, o_ref, *, alpha):
    gm = jax.nn.sigmoid(gm_ref[...])
    d = D_MODEL
    mix_in = (gm[:, 0:d] * _dot(oa_ref[...], pa_ref[...]) + gm[:, d:2 * d] * _dot(ob_ref[...], pb_ref[...])
              + gm[:, 2 * d:3 * d] * _dot(oc_ref[...], pc_ref[...]))
    mix = _dot(mix_in.astype(BF), wo_ref[...])
    o_ref[...] = _layer_norm(alpha * x_ref[...] + mix, g_ref[...], b_ref[...])


def _merge(x, oa, ob, oc, u, pa, pb, pc, wo, g, b, tm, alpha):
    m = x.shape[0]
    row = lambda i: (i, 0)
    fix = lambda i: (0, 0)
    return pl.pallas_call(
        functools.partial(_merge_kernel, alpha=alpha),
        grid=(m // tm,),
        in_specs=[pl.BlockSpec((tm, D_MODEL), row),
                  pl.BlockSpec((tm, DN_W), row), pl.BlockSpec((tm, CF_C), row), pl.BlockSpec((tm, NSA_W), row),
                  pl.BlockSpec((tm, 3 * D_MODEL), lambda i: (i, P_MERGE // (3 * D_MODEL))),
                  pl.BlockSpec((DN_W, D_MODEL), fix), pl.BlockSpec((CF_C, D_MODEL), fix),
                  pl.BlockSpec((NSA_W, D_MODEL), fix), pl.BlockSpec((D_MODEL, D_MODEL), fix),
                  pl.BlockSpec((1, D_MODEL), fix), pl.BlockSpec((1, D_MODEL), fix)],
        out_specs=pl.BlockSpec((tm, D_MODEL), row),
        out_shape=jax.ShapeDtypeStruct((m, D_MODEL), F32),
        compiler_params=_params("arbitrary", vmem_mb=48),
        name="merge_ln",
    )(x, oa, ob, oc, u, pa, pb, pc, wo, g, b)


def _ffn_kernel(x_ref, wi_ref, wo_ref, g_ref, b_ref, o_ref, acc_ref, *, alpha):
    j = pl.program_id(1)
    fc = D_FF // FFN_CHUNKS
    x = x_ref[...]
    h = _dot(x.astype(BF), wi_ref[0])
    y = _dot((_silu(h[:, :fc]) * h[:, fc:]).astype(BF), wo_ref[0])

    @pl.when(j == 0)
    def _():
        acc_ref[...] = y

    @pl.when(j > 0)
    def _():
        acc_ref[...] += y

    @pl.when(j == FFN_CHUNKS - 1)
    def _():
        o_ref[...] = _layer_norm(alpha * x + acc_ref[...], g_ref[...], b_ref[...])


def _ffn(x, wi, wo, g, b, tm, alpha):
    m = x.shape[0]
    fc = D_FF // FFN_CHUNKS
    return pl.pallas_call(
        functools.partial(_ffn_kernel, alpha=alpha),
        grid=(m // tm, FFN_CHUNKS),
        in_specs=[pl.BlockSpec((tm, D_MODEL), lambda i, j: (i, 0)),
                  pl.BlockSpec((1, D_MODEL, 2 * fc), lambda i, j: (j, 0, 0)),
                  pl.BlockSpec((1, fc, D_MODEL), lambda i, j: (j, 0, 0)),
                  pl.BlockSpec((1, D_MODEL), lambda i, j: (0, 0)),
                  pl.BlockSpec((1, D_MODEL), lambda i, j: (0, 0))],
        out_specs=pl.BlockSpec((tm, D_MODEL), lambda i, j: (i, 0)),
        out_shape=jax.ShapeDtypeStruct((m, D_MODEL), F32),
        scratch_shapes=[pltpu.VMEM((tm, D_MODEL), F32)],
        compiler_params=_params("arbitrary", "arbitrary", vmem_mb=48),
        name="ffn_ln",
    )(x, wi, wo, g, b)


def _sample_mix_kernel(qkv_ref, z_ref, sm_ref, cf_ref, st_ref, dcs_ref, cfs_ref, cw_ref, par_ref, ng_ref,
                       fw_ref, fbias_ref, fg_ref, fb_ref, oa_ref, ob_ref, so_ref, glu_ref):
    cs = dcs_ref[0, 0]
    new = qkv_ref[0]
    cw = cw_ref[...]
    conv = cw[DN_CONV - 1:DN_CONV, :] * new
    for k in range(DN_CONV - 1):
        conv = conv + cw[k:k + 1, :] * cs[k:k + 1, :]
    h = _silu(conv)
    beta_all, g_all = _dn_gates(sm_ref[0], par_ref[...])
    z = z_ref[0]
    ng = ng_ref[...]
    outs = []
    for hd in range(DN_H):
        qh = _l2norm(h[:, hd * DN_DK:(hd + 1) * DN_DK]) * (DN_DK ** -0.5)
        kh = _l2norm(h[:, DN_QK + hd * DN_DK:DN_QK + (hd + 1) * DN_DK])
        vh = h[:, 2 * DN_QK + hd * DN_DV:2 * DN_QK + (hd + 1) * DN_DV]
        beta = beta_all[:, SM_B + hd:SM_B + hd + 1]
        g = g_all[:, SM_A + hd:SM_A + hd + 1]
        kcol = jnp.broadcast_to(kh, (DN_DK, DN_DK)).T
        qcol = jnp.broadcast_to(qh, (DN_DK, DN_DK)).T
        s = jnp.exp(g) * st_ref[0, 0, hd]
        u = beta * (vh - jnp.sum(s * kcol, 0, keepdims=True))
        s = s + kcol * u
        so_ref[0, hd] = s
        o = jnp.sum(s * qcol, 0, keepdims=True)
        o = o * lax.rsqrt(jnp.mean(o * o, -1, keepdims=True) + 1e-6) * ng
        outs.append(o * _silu(z[:, hd * DN_DV:(hd + 1) * DN_DV]))
    oa_ref[0] = jnp.concatenate(outs, axis=1).astype(BF)

    ucf = cf_ref[0]
    glu = ucf[:, :CF_C] * jax.nn.sigmoid(ucf[:, CF_C:])
    glu_ref[0] = glu
    fw = fw_ref[...]
    acc = fbias_ref[...] + jnp.sum(fw[0:CF_K - 1, :] * cfs_ref[0, 0], 0, keepdims=True) + fw[CF_K - 1:CF_K, :] * glu
    ob_ref[0] = _silu(_layer_norm(acc, fg_ref[...], fb_ref[...])).astype(BF)


def _sample_mix(u_s3, state_dn, state_dn_conv, state_cf_conv, layer, conv_w, par, ng, fw, fbias, fg, fb, db):
    fix = lambda b: (0, 0)
    return pl.pallas_call(
        _sample_mix_kernel,
        grid=(db,),
        in_specs=[pl.BlockSpec((1, 1, DN_CONV_C), lambda b: (b, 0, P_DNQKV // DN_CONV_C)),
                  pl.BlockSpec((1, 1, DN_W), lambda b: (b, 0, P_DNZ // DN_W)),
                  pl.BlockSpec((1, 1, 128), lambda b: (b, 0, P_SMALL // 128)),
                  pl.BlockSpec((1, 1, 2 * CF_C), lambda b: (b, 0, P_CF // (2 * CF_C))),
                  pl.BlockSpec((1, 1, DN_H, DN_DK, DN_DV), lambda b: (layer, b, 0, 0, 0)),
                  pl.BlockSpec((1, 1, DN_CONV - 1, DN_CONV_C), lambda b: (layer, b, 0, 0)),
                  pl.BlockSpec((1, 1, CF_K - 1, CF_C), lambda b: (layer, b, 0, 0)),
                  pl.BlockSpec((DN_CONV, DN_CONV_C), fix),
                  pl.BlockSpec((8, 128), fix),
                  pl.BlockSpec((1, DN_DV), fix),
                  pl.BlockSpec((CF_K, CF_C), fix),
                  pl.BlockSpec((1, CF_C), fix), pl.BlockSpec((1, CF_C), fix), pl.BlockSpec((1, CF_C), fix)],
        out_specs=[pl.BlockSpec((1, 1, DN_W), lambda b: (b, 0, 0)),
                   pl.BlockSpec((1, 1, CF_C), lambda b: (b, 0, 0)),
                   pl.BlockSpec((1, DN_H, DN_DK, DN_DV), lambda b: (b, 0, 0, 0)),
                   pl.BlockSpec((1, 1, CF_C), lambda b: (b, 0, 0))],
        out_shape=[jax.ShapeDtypeStruct((db, 1, DN_W), BF),
                   jax.ShapeDtypeStruct((db, 1, CF_C), BF),
                   jax.ShapeDtypeStruct((db, DN_H, DN_DK, DN_DV), F32),
                   jax.ShapeDtypeStruct((db, 1, CF_C), F32)],
        compiler_params=_params("arbitrary"),
        name="sample_dn_cf",
    )(u_s3, u_s3, u_s3, u_s3, state_dn, state_dn_conv, state_cf_conv, conv_w, par, ng, fw, fbias, fg, fb)


def _sample_q4(q, g):
    rows = [q[:, (g * NSA_HPG + hp) * NSA_DH:(g * NSA_HPG + hp + 1) * NSA_DH] for hp in range(NSA_HPG)]
    return jnp.concatenate(rows + [jnp.zeros((8 - NSA_HPG, NSA_DH), F32)], axis=0)


def _sample_sel_kernel(q_ref, kvc_ref, ocmp_ref, idx_ref, sc_ref, *, t_pos, n_sel, n_sel_pad, db):
    b = pl.program_id(0)
    q = q_ref[0] * (NSA_DH ** -0.5)
    kvc = kvc_ref[0]
    nc = kvc.shape[0]
    c_end = (lax.broadcasted_iota(jnp.int32, (1, nc), 1) + 1) * L_CMP - 1
    pair = _pair_matrix(nc, n_sel_pad)
    blk = lax.broadcasted_iota(jnp.int32, (1, n_sel_pad), 1)
    cur = t_pos // L_SEL
    valid = (blk < n_sel) & (blk * L_SEL <= t_pos)
    forced = (blk == 0) | (blk == cur) | (blk == cur - 1)
    o_parts = []
    scores = []
    for g in range(NSA_G):
        q4 = _sample_q4(q, g).astype(BF)
        kcol = slice(g * NSA_DH, (g + 1) * NSA_DH)
        vcol = slice(NSA_G * NSA_DH + g * NSA_DH, NSA_G * NSA_DH + (g + 1) * NSA_DH)
        s_c = _dot_nt(q4, kvc[:, kcol].astype(BF))
        p_c = _masked_softmax(s_c, c_end <= t_pos)
        o_cmp = _dot(p_c.astype(BF), kvc[:, vcol].astype(BF))
        imp_c = p_c[0:1] + p_c[1:2] + p_c[2:3] + p_c[3:4]
        imp = _dot(jnp.broadcast_to(imp_c, (8, nc)), pair, precision=HI)[0:1]
        scores.append(jnp.where(valid, jnp.where(forced, FORCE_SCORE, imp), -jnp.inf))
        o_parts += [o_cmp[hp:hp + 1, :] for hp in range(NSA_HPG)]
    ocmp_ref[0] = jnp.concatenate(o_parts, axis=1)
    sc_ref[pl.ds(b, 1)] = jnp.concatenate(scores, axis=0)[None]

    @pl.when(b == db - 1)
    def _():
        score = sc_ref[...]
        blk3 = lax.broadcasted_iota(jnp.int32, score.shape, 2)
        lane = lax.broadcasted_iota(jnp.int32, (db, NSA_G, 128), 2)
        idx = jnp.full((db, NSA_G, 128), -1, jnp.int32)
        for r in range(min(N_TOP, n_sel)):
            m = jnp.max(score, -1, keepdims=True)
            first = jnp.min(jnp.where(score == m, blk3, n_sel_pad), -1, keepdims=True)
            idx = jnp.where((lane == r) & (m > -jnp.inf), first, idx)
            score = jnp.where(blk3 == first, -jnp.inf, score)
        idx_ref[...] = idx


def _sample_select(u_s3, kvc_s, db, t_pos, n_sel):
    nc = kvc_s.shape[1]
    n_sel_pad = -(-n_sel // 128) * 128
    return pl.pallas_call(
        functools.partial(_sample_sel_kernel, t_pos=t_pos, n_sel=n_sel, n_sel_pad=n_sel_pad, db=db),
        grid=(db,),
        in_specs=[pl.BlockSpec((1, 1, NSA_W), lambda b: (b, 0, P_NQ // NSA_W)),
                  pl.BlockSpec((1, nc, NSA_ROW), lambda b: (b, 0, 0))],
        out_specs=[pl.BlockSpec((1, 1, NSA_W), lambda b: (b, 0, 0)),
                   pl.BlockSpec((db, NSA_G, 128), lambda b: (0, 0, 0))],
        out_shape=[jax.ShapeDtypeStruct((db, 1, NSA_W), F32),
                   jax.ShapeDtypeStruct((db, NSA_G, 128), jnp.int32)],
        scratch_shapes=[pltpu.VMEM((db, NSA_G, n_sel_pad), F32)],
        compiler_params=_params("arbitrary"),
        name="nsa_sample_select",
    )(u_s3, kvc_s)


def _sample_attn_kernel(idx_ref, pt_ref, *refs, t_pos, n_past_blk, l_buf):
    nblk = NSA_G * N_TOP
    blks = refs[:nblk]
    q_ref, sm_ref, nkv_ref, win_ref, ocmp_ref, o_ref = refs[nblk:]
    b = pl.program_id(0)
    q = q_ref[0] * (NSA_DH ** -0.5)
    gates = jax.nn.sigmoid(sm_ref[0])
    newkv = nkv_ref[0]
    ocmp = ocmp_ref[0]
    sub_pp = PAGE_SIZE // L_SEL
    nkeys = N_TOP * PAGE_SIZE
    lane = lax.broadcasted_iota(jnp.int32, (1, nkeys), 1)
    kslot = lane // PAGE_SIZE
    prow = lane % PAGE_SIZE
    new_visible = n_past_blk * L_SEL <= t_pos
    wpos = t_pos - l_buf + lax.broadcasted_iota(jnp.int32, (1, l_buf), 1)
    wdist = t_pos - wpos
    m_w = (wdist >= 0) & (wdist <= WINDOW) & (wpos >= 0)

    def with_new_key(s, s_n, vt, v_n):
        m = jnp.maximum(jnp.max(s, -1, keepdims=True), s_n)
        m = jnp.where(m > -jnp.inf, m, 0.0)
        e = jnp.exp(s - m)
        e_n = jnp.exp(s_n - m)
        den = jnp.sum(e, -1, keepdims=True) + e_n
        den = jnp.where(den > 0, den, 1.0)
        return _dot_nt((e / den).astype(BF), vt) + (e_n / den).astype(BF).astype(F32) * v_n.astype(BF).astype(F32)

    outs = []
    for g in range(NSA_G):
        q4 = _sample_q4(q, g).astype(BF)
        q4f = q4.astype(F32)
        kts, vts = [], []
        idxv = jnp.zeros((1, nkeys), jnp.int32)
        any_new = False
        for k in range(N_TOP):
            ik = idx_ref[(b * NSA_G + g) * N_TOP + k]
            kts.append(blks[g * N_TOP + k][0, 0, 0, 0])
            vts.append(blks[g * N_TOP + k][0, 0, 1, 0])
            idxv = jnp.where(kslot == k, ik, idxv)
            any_new = jnp.logical_or(any_new, ik >= n_past_blk)
        kmat = jnp.concatenate(kts, axis=1).astype(BF)
        vmat = jnp.concatenate(vts, axis=1).astype(BF)
        pi = jnp.clip(idxv, 0, n_past_blk - 1)
        in_half = (prow // L_SEL) == (pi % sub_pp)
        pos = idxv * L_SEL + prow % L_SEL
        mask = (idxv >= 0) & (idxv < n_past_blk) & in_half & (pos <= t_pos)
        s = jnp.where(mask, _dot(q4, kmat), -jnp.inf)
        c0 = NSA_ROW + g * NSA_DH
        new_k = newkv[:, c0:c0 + NSA_DH]
        new_v = newkv[:, c0 + NSA_G * NSA_DH:c0 + NSA_G * NSA_DH + NSA_DH]
        s_n = jnp.sum(q4f * new_k.astype(BF).astype(F32), -1, keepdims=True)
        s_n = jnp.where(jnp.logical_and(any_new, new_visible), s_n, -jnp.inf)
        o_slc = with_new_key(s, s_n, vmat, new_v)
        c1 = 2 * NSA_ROW + g * NSA_DH
        nwk = newkv[:, c1:c1 + NSA_DH]
        nwv = newkv[:, c1 + NSA_G * NSA_DH:c1 + NSA_G * NSA_DH + NSA_DH]
        s_w = jnp.where(m_w, _dot(q4, win_ref[0, 0, 0, g].astype(BF)), -jnp.inf)
        s_wn = jnp.sum(q4f * nwk.astype(BF).astype(F32), -1, keepdims=True)
        o_win = with_new_key(s_w, s_wn, win_ref[0, 0, 1, g].astype(BF), nwv)
        for hp in range(NSA_HPG):
            hcol = slice((g * NSA_HPG + hp) * NSA_DH, (g * NSA_HPG + hp + 1) * NSA_DH)
            outs.append(_head_gate(gates, g, hp, 0) * ocmp[:, hcol] + _head_gate(gates, g, hp, 1) * o_slc[hp:hp + 1]
                        + _head_gate(gates, g, hp, 2) * o_win[hp:hp + 1])
    o_ref[0] = jnp.concatenate(outs, axis=1).astype(BF)


def _sample_attn(idx_flat, pt_flat, slc_t, u_s3, win_t, ocmp, layer, db, n_pages, t_pos, l_buf):
    sub_pp = PAGE_SIZE // L_SEL
    n_past_blk = n_pages * sub_pp

    def blk_map(g, k):
        def f(b, idx, pt):
            pi = jnp.clip(idx[(b * NSA_G + g) * N_TOP + k], 0, n_past_blk - 1)
            return (layer, pt[b * n_pages + pi // sub_pp], 0, g, 0, 0)
        return f

    fix3 = lambda col: (lambda b, idx, pt: (b, 0, col))
    grid_spec = pltpu.PrefetchScalarGridSpec(
        num_scalar_prefetch=2,
        grid=(db,),
        in_specs=[pl.BlockSpec((1, 1, 2, 1, NSA_DH, PAGE_SIZE), blk_map(g, k))
                  for g in range(NSA_G) for k in range(N_TOP)]
        + [pl.BlockSpec((1, 1, NSA_W), fix3(P_NQ // NSA_W)),
           pl.BlockSpec((1, 1, 128), fix3(P_SMALL // 128)),
           pl.BlockSpec((1, 1, NSA_KV), fix3(P_NKV // NSA_KV)),
           pl.BlockSpec((1, 1, 2, NSA_G, NSA_DH, l_buf), lambda b, idx, pt: (layer, b, 0, 0, 0, 0)),
           pl.BlockSpec((1, 1, NSA_W), fix3(0))],
        out_specs=pl.BlockSpec((1, 1, NSA_W), fix3(0)),
    )
    return pl.pallas_call(
        functools.partial(_sample_attn_kernel, t_pos=t_pos, n_past_blk=n_past_blk, l_buf=l_buf),
        grid_spec=grid_spec,
        out_shape=jax.ShapeDtypeStruct((db, 1, NSA_W), BF),
        compiler_params=_params("arbitrary"),
        name="nsa_sample_attn",
    )(idx_flat, pt_flat, *([slc_t] * (NSA_G * N_TOP)), u_s3, u_s3, u_s3, win_t, ocmp)


def _pack_w_in(w):
    zeros = jnp.zeros((w.shape[0], 128 - 2 * DN_H - 3 * NSA_H + 128), w.dtype)
    return jnp.concatenate(
        [w[:, OFF_MERGE:N_IN], w[:, 0:DN_CONV_C], w[:, OFF_NSA_KV:OFF_NSA_G], w[:, OFF_DN_B:OFF_CF],
         w[:, OFF_NSA_G:OFF_MERGE], zeros, w[:, OFF_DN_Z:OFF_DN_B], w[:, OFF_CF:OFF_NSA_Q],
         w[:, OFF_NSA_Q:OFF_NSA_KV]], axis=1).astype(BF)


def _pack_cmp_weights(pe, w1, w2):
    eye = jnp.eye(2, dtype=F32)
    w1r = w1.reshape(2, L_CMP, NSA_DH, NSA_DH)
    w1big = jnp.einsum("slde,sp,gq->lsgdpqe", w1r, eye, eye).reshape(CMP_FLAT, NSA_ROW).astype(BF)
    w2big = jnp.einsum("sde,sp,gq->sgdpqe", w2, eye, eye).reshape(NSA_ROW, NSA_ROW).astype(BF)
    pe_flat = jnp.broadcast_to(pe.transpose(1, 0, 2)[:, :, None, :], (L_CMP, 2, NSA_G, NSA_DH)).reshape(1, CMP_FLAT)
    return pe_flat, w1big, w2big


def _pack_dn_par(a_log, dt_bias):
    par = jnp.zeros((8, 128), F32)
    par = par.at[0, SM_A:SM_A + DN_H].set(a_log)
    return par.at[1, SM_A:SM_A + DN_H].set(dt_bias)


def _pack_ffn(w_in, w_out):
    fc = D_FF // FFN_CHUNKS
    wi = jnp.stack([jnp.concatenate([w_in[:, c * fc:(c + 1) * fc], w_in[:, D_FF + c * fc:D_FF + (c + 1) * fc]], axis=1)
                    for c in range(FFN_CHUNKS)]).astype(BF)
    wo = w_out.reshape(FFN_CHUNKS, fc, D_MODEL).astype(BF)
    return wi, wo


def kernel(x_prompt, x_sample, cache_nsa_cmp, cache_nsa_slc, cache_nsa_win, state_dn, state_dn_conv, state_cf_conv, page_table, w_in, dn_conv_w, dn_a_log, dn_dt_bias, dn_norm_g, cf_dw_w, cf_dw_b, cf_ln_g, cf_ln_b, nsa_cmp_pe, nsa_cmp_w1, nsa_cmp_w2, w_branch_a, w_branch_b, w_branch_c, w_out, ln1_g, ln1_b, ln2_g, ln2_b, w_ffn_in, w_ffn_out):
    bsz, t, d = x_prompt.shape
    db = x_sample.shape[0]
    depth = w_in.shape[0]
    n_pool = cache_nsa_cmp.shape[1]
    n_pages = page_table.shape[1]
    past = n_pages * PAGE_SIZE
    l_buf = cache_nsa_win.shape[2]
    alpha = _alpha(depth)
    assert x_sample.shape[1] == 1 and d == D_MODEL
    assert t % CF_TILE == 0 and t >= WINDOW + Q_BLOCK and t % SEL_TK == 0 and n_pages % min(CMP_PAGES, n_pages) == 0

    cmp_t = jnp.transpose(cache_nsa_cmp, (0, 1, 3, 4, 5, 2)).reshape(depth, n_pool, NSA_ROW, PAGE_SIZE)
    slc_t = jnp.transpose(cache_nsa_slc, (0, 1, 3, 4, 5, 2))
    win_t = jnp.transpose(cache_nsa_win, (0, 1, 3, 4, 5, 2))
    pt_flat = page_table.reshape(-1)
    tm_p = 512 if (bsz * t) % 512 == 0 else 256
    wlen = min(WINDOW, t)

    xp = x_prompt.reshape(bsz * t, d)
    xs = x_sample.reshape(db, d)
    st = {k: [] for k in ("cmp_p", "cmp_s", "slc_p", "slc_s", "win_p", "win_s", "dn_p", "dn_s", "dnc_p", "dnc_s",
                          "cfc_p", "cfc_s")}
    for l in range(depth):
        wr = _pack_w_in(w_in[l])
        w_qkv_t = w_in[l][:, OFF_NSA_Q:OFF_NSA_G].T.astype(BF)
        pe_flat, w1big, w2big = _pack_cmp_weights(nsa_cmp_pe[l], nsa_cmp_w1[l], nsa_cmp_w2[l])
        par = _pack_dn_par(dn_a_log[l], dn_dt_bias[l])
        ng = dn_norm_g[l].reshape(1, DN_DV)
        fbias, fg, fb = cf_dw_b[l].reshape(1, CF_C), cf_ln_g[l].reshape(1, CF_C), cf_ln_b[l].reshape(1, CF_C)
        pa, pb, pc, wo = (w_branch_a[l].astype(BF), w_branch_b[l].astype(BF), w_branch_c[l].astype(BF),
                          w_out[l].astype(BF))
        wi_f, wo_f = _pack_ffn(w_ffn_in[l], w_ffn_out[l])
        g1, b1, g2, b2 = (ln1_g[l].reshape(1, d), ln1_b[l].reshape(1, d), ln2_g[l].reshape(1, d),
                          ln2_b[l].reshape(1, d))

        u_p = _matmul(xp, wr, 1024 if (bsz * t) % 1024 == 0 else tm_p, 1536)
        u_p3 = u_p.reshape(bsz, t, N_PAD)
        qkvt_bf, kvt = _proj_nt(xp.reshape(bsz, t, d), w_qkv_t, 512)
        oa, dn_state = _dn_prompt(u_p3, dn_conv_w[l], par, ng)
        ob, cf_tail = _cf_prompt(u_p, cf_dw_w[l], fbias, fg, fb, bsz, t)
        kvc, kvct = _compress_prompt(u_p[:, P_NKV:P_NKV + NSA_ROW].reshape(bsz, t // L_CMP, CMP_FLAT), pe_flat, w1big,
                                     w2big)
        kv_bf = u_p3[:, :, P_NKV:P_NKV + NSA_KV].astype(BF)
        oc = _nsa_prompt(u_p, qkvt_bf, kvc, kvct, kv_bf, bsz, t)
        x1 = _merge(xp, oa.reshape(bsz * t, DN_W), ob, oc, u_p, pa, pb, pc, wo, g1, b1, 256, alpha)
        xp_next = _ffn(x1, wi_f, wo_f, g2, b2, tm_p, alpha)
        kvt6 = kvt.reshape(bsz, 3, 2, NSA_G, NSA_DH, t)
        st["cmp_p"].append(jnp.transpose(kvt6[:, 0], (0, 4, 1, 2, 3)))
        st["slc_p"].append(jnp.transpose(kvt6[:, 1], (0, 4, 1, 2, 3)))
        st["win_p"].append(jnp.transpose(kvt6[:, 2, :, :, :, t - wlen:], (0, 4, 1, 2, 3)))
        st["dn_p"].append(dn_state)
        st["dnc_p"].append(u_p3[:, t - (DN_CONV - 1):, P_DNQKV:P_DNQKV + DN_CONV_C])
        st["cfc_p"].append(cf_tail[:, CF_HALO - (CF_K - 1):])

        u_s = _matmul(xs, wr, db, 1536)
        u_s3 = u_s.reshape(db, 1, N_PAD)
        oa_s, ob_s, dn_state_s, glu_s = _sample_mix(u_s3, state_dn, state_dn_conv, state_cf_conv, l, dn_conv_w[l],
                                                    par, ng, cf_dw_w[l], fbias, fg, fb, db)
        pe_t = jnp.broadcast_to(jnp.tile(nsa_cmp_pe[l].transpose(0, 2, 1), (1, 1, BLK_PER_PAGE))[:, None],
                                (2, NSA_G, NSA_DH, PAGE_SIZE)).reshape(NSA_ROW, PAGE_SIZE)
        kvc_s = _compress_sample(cmp_t, pt_flat, l, pe_t, w1big.reshape(L_CMP, NSA_ROW, NSA_ROW), w2big, db,
                                 n_pages)
        n_sel = n_pages * (PAGE_SIZE // L_SEL) + 1
        ocmp_s, idx = _sample_select(u_s3, kvc_s, db, past, n_sel)
        oc_s = _sample_attn(idx[:, :, :N_TOP].reshape(-1), pt_flat, slc_t, u_s3, win_t, ocmp_s, l, db, n_pages,
                            past, l_buf)
        x1s = _merge(xs, oa_s.reshape(db, DN_W), ob_s.reshape(db, CF_C), oc_s.reshape(db, NSA_W), u_s, pa, pb, pc,
                     wo, g1, b1, db, alpha)
        xs_next = _ffn(x1s, wi_f, wo_f, g2, b2, db, alpha)
        kv_s = u_s[:, P_NKV:P_NKV + NSA_KV].reshape(db, 1, 3, 2, NSA_G, NSA_DH)
        st["cmp_s"].append(kv_s[:, :, 0])
        st["slc_s"].append(kv_s[:, :, 1])
        st["win_s"].append(jnp.concatenate([cache_nsa_win[l], kv_s[:, :, 2]], axis=1)[:, -min(WINDOW, l_buf + 1):])
        st["dn_s"].append(dn_state_s)
        st["dnc_s"].append(jnp.concatenate([state_dn_conv[l], u_s[:, None, P_DNQKV:P_DNQKV + DN_CONV_C]],
                                           axis=1)[:, -(DN_CONV - 1):])
        st["cfc_s"].append(jnp.concatenate([state_cf_conv[l], glu_s], axis=1)[:, -(CF_K - 1):])
        xp, xs = xp_next, xs_next

    stk = {k: jnp.stack(v) for k, v in st.items()}
    return (xp.reshape(bsz, t, d), xs.reshape(db, 1, d), stk["cmp_p"], stk["cmp_s"], stk["slc_p"], stk["slc_s"],
            stk["win_p"], stk["win_s"], stk["dn_p"], stk["dn_s"], stk["dnc_p"], stk["dnc_s"], stk["cfc_p"],
            stk["cfc_s"])
```
